```python
import functools
import jax
import jax.numpy as jnp
from jax import lax
import numpy as np

D_MODEL = 1024
BATCH = 8
SEQ = 2048
DEPTH = 4
DEC_BATCH = 128
DEC_SEQ = 8
PAST_LEN = 8192
PAGE_SIZE = 128

N_HEADS_A = 8
NOPE_DIM = 64
ROPE_DIM = 32
V_DIM = 64
Q_RANK = 768
KV_RANK = 256
ROPE_THETA = 10000.0
Q_BLOCK = 128
MLA_SCALE = (NOPE_DIM + ROPE_DIM) ** -0.5
W_B = 512
CONV_W = 3
W_C = 512
N_GROUPS_C = 4
CHUNK = 128
N_KEYS = 128
N_EXPERTS = N_KEYS * N_KEYS
N_HEADS_P = 8
D_KEY = 256
TOPK_P = 16
PEER_BLOCK = 128
D_PLE = 256
EPS = 1e-6

IN_SPLITS = (Q_RANK, KV_RANK, ROPE_DIM, W_B, W_B, W_B, W_C, W_C, D_MODEL, D_MODEL, D_MODEL)
N_IN = sum(IN_SPLITS)

kernel_name = 'hybrid_mla_conv_chunkmlp_peer_step'


def _split_offsets():
    offs, acc = [], 0
    for s in IN_SPLITS[:-1]:
        acc += s
        offs.append(acc)
    return offs


def _rms(x, g):
    xf = x.astype(jnp.float32)
    y = xf * lax.rsqrt(jnp.mean(xf * xf, axis=-1, keepdims=True) + EPS)
    return (y * g.astype(jnp.float32)).astype(x.dtype)


def _layernorm(x, g, b):
    xf = x.astype(jnp.float32)
    xc = xf - jnp.mean(xf, axis=-1, keepdims=True)
    y = xc * lax.rsqrt(jnp.mean(xc * xc, axis=-1, keepdims=True) + EPS)
    return (y * g.astype(jnp.float32) + b.astype(jnp.float32)).astype(x.dtype)


def _rope_tables(pos):
    inv = 1.0 / (ROPE_THETA ** (jnp.arange(0, ROPE_DIM, 2, dtype=jnp.float32) / ROPE_DIM))
    ang = pos.astype(jnp.float32)[:, None] * inv[None, :]
    return jnp.cos(ang), jnp.sin(ang)


def _apply_rope(x, cos, sin):
    xf = x.astype(jnp.float32)
    x1, x2 = xf[..., :ROPE_DIM // 2], xf[..., ROPE_DIM // 2:]
    return jnp.concatenate([x1 * cos - x2 * sin, x2 * cos + x1 * sin], axis=-1).astype(x.dtype)


def _mla_project(cq, ckv, kpe, pos, g_q, g_kv, w_uq, w_uk):
    cq = _rms(cq, g_q)
    ckv = _rms(ckv, g_kv)
    q = jnp.einsum('btr,rhd->bthd', cq, w_uq)
    q_nope, q_pe = q[..., :NOPE_DIM], q[..., NOPE_DIM:]
    cos, sin = _rope_tables(pos)
    q_pe = _apply_rope(q_pe, cos[None, :, None, :], sin[None, :, None, :])
    kpe = _apply_rope(kpe, cos[None], sin[None])
    q_lat = jnp.einsum('bthd,rhd->bthr', q_nope, w_uk)
    return q_lat, q_pe, ckv, kpe


def _mla_attend(q_lat, q_pe, ckv, kpe, mask):
    s = jnp.einsum('bqhr,bkr->bhqk', q_lat, ckv) + jnp.einsum('bqhd,bkd->bhqk', q_pe, kpe)
    s = jnp.where(mask[None, None], s.astype(jnp.float32) * MLA_SCALE, -jnp.inf)
    p = jax.nn.softmax(s, axis=-1).astype(ckv.dtype)
    return jnp.einsum('bhqk,bkr->bqhr', p, ckv)


def _mla_prompt(q_lat, q_pe, ckv, kpe):
    b, t = q_lat.shape[:2]
    nb = t // Q_BLOCK
    ql = jnp.moveaxis(q_lat.reshape(b, nb, Q_BLOCK, N_HEADS_A, KV_RANK), 1, 0)
    qp = jnp.moveaxis(q_pe.reshape(b, nb, Q_BLOCK, N_HEADS_A, ROPE_DIM), 1, 0)
    kpos = jnp.arange(t, dtype=jnp.int32)

    def block(args):
        i, qlb, qpb = args
        qpos = i * Q_BLOCK + jnp.arange(Q_BLOCK, dtype=jnp.int32)
        return _mla_attend(qlb, qpb, ckv, kpe, kpos[None, :] <= qpos[:, None])

    o = lax.map(block, (jnp.arange(nb, dtype=jnp.int32), ql, qp))
    return jnp.moveaxis(o, 0, 1).reshape(b, t, N_HEADS_A, KV_RANK)


def _mla_sample(q_lat, q_pe, ckv, kpe, past_ckv, past_kpe):
    past = past_ckv.shape[1]
    t = q_lat.shape[1]
    keys_c = jnp.concatenate([past_ckv, ckv], axis=1)
    keys_p = jnp.concatenate([past_kpe, kpe], axis=1)
    qpos = past + jnp.arange(t, dtype=jnp.int32)
    kpos = jnp.arange(past + t, dtype=jnp.int32)
    return _mla_attend(q_lat, q_pe, keys_c, keys_p, kpos[None, :] <= qpos[:, None])


def _short_conv(gb, gc, hc, prev, w_conv):
    xc = gc * hc
    xp = jnp.concatenate([prev.astype(xc.dtype), xc], axis=1)
    t = xc.shape[1]
    y = xp[:, 0:t] * w_conv[0]
    for k in range(1, CONV_W):
        y = y + xp[:, k:k + t] * w_conv[k]
    return gb * y, xp[:, -(CONV_W - 1):]


def _chunk_mlp(u, v, g_v, b_v, w_s, b_s):
    b, t = u.shape[:2]
    vn = _layernorm(v, g_v, b_v)
    ln = min(t, CHUNK)
    ws = jnp.tril(w_s[:, :ln, :ln])
    vg = vn.reshape(b, t // ln, ln, N_GROUPS_C, W_C // N_GROUPS_C)
    mix = jnp.einsum('gts,bnsgd->bntgd', ws, vg) + b_s[:, :ln].T[None, None, :, :, None]
    return u * mix.reshape(b, t, W_C), vn


def _peer(h, w_pq, sub_keys, u_tab, v_tab):
    b, t, d = h.shape
    x = h.reshape(b * t, d)
    n = x.shape[0]
    nb = -(-n // PEER_BLOCK)
    xp = jnp.pad(x, ((0, nb * PEER_BLOCK - n), (0, 0))).reshape(nb, PEER_BLOCK, d)

    def block(xb):
        q = jnp.einsum('td,dhcr->thcr', xb, w_pq)
        s = jnp.einsum('thcr,hckr->thck', q, sub_keys).astype(jnp.float32)
        s_top, i_top = lax.top_k(s, TOPK_P)
        cand = s_top[:, :, 0, :, None] + s_top[:, :, 1, None, :]
        cand_idx = i_top[:, :, 0, :, None] * N_KEYS + i_top[:, :, 1, None, :]
        nt = xb.shape[0]
        best, sel = lax.top_k(cand.reshape(nt, N_HEADS_P, TOPK_P * TOPK_P), TOPK_P)
        idx = jnp.take_along_axis(cand_idx.reshape(nt, N_HEADS_P, TOPK_P * TOPK_P), sel, axis=-1)
        g = jax.nn.softmax(best, axis=-1).astype(xb.dtype)
        act = jax.nn.gelu(jnp.einsum('thkd,td->thk', u_tab[idx], xb), approximate=False)
        return jnp.einsum('thk,thkd->td', g * act, v_tab[idx])

    y = lax.map(block, xp).reshape(nb * PEER_BLOCK, d)[:n]
    return y.reshape(b, t, d)


def _layer(x, p_i, pos, conv_prev, attend, g_attn, w_in, g_q, g_kv, w_uq, w_uk, w_uv, w_conv,
           g_v, b_v, w_s, b_s, w_br_a, w_br_b, w_br_c, w_out, g_ffn, w_pq, sub_keys, u_tab, v_tab,
           g_ple, w_pg, w_ple):
    b, t, _ = x.shape
    h = _rms(x, g_attn)
    z = jnp.einsum('btd,dn->btn', h, w_in)
    cq, ckv, kpe, cb, cc, ch, u, v, ga, gb, gc = jnp.split(z, _split_offsets(), axis=-1)
    q_lat, q_pe, ckv, kpe = _mla_project(cq, ckv, kpe, pos, g_q, g_kv, w_uq, w_uk)
    o_lat = attend(q_lat, q_pe, ckv, kpe)
    a = jnp.einsum('bthr,rhd->bthd', o_lat, w_uv).reshape(b, t, N_HEADS_A * V_DIM)
    bconv, conv_state = _short_conv(cb, cc, ch, conv_prev, w_conv)
    cmix, vn = _chunk_mlp(u, v, g_v, b_v, w_s, b_s)
    merged = (jax.nn.sigmoid(ga) * (a @ w_br_a)
              + jax.nn.sigmoid(gb) * (bconv @ w_br_b)
              + jax.nn.sigmoid(gc) * (cmix @ w_br_c))
    x = x + merged @ w_out
    x = x + _peer(_rms(x, g_ffn), w_pq, sub_keys, u_tab, v_tab)
    x = x + jax.nn.sigmoid(_rms(x, g_ple) @ w_pg) * (p_i @ w_ple)
    return x, ckv, kpe, conv_state, vn


def setup_inputs(seed: int = 0) -> dict:
    key = jax.random.key(seed)
    ks = iter(jax.random.split(key, 40))

    def nrm(shape, scale):
        return jax.random.normal(next(ks), shape, jnp.float32) * scale

    def gain(shape):
        return 1.0 + nrm(shape, 0.01)

    n_pages = PAST_LEN // PAGE_SIZE
    n_used = DEC_BATCH * n_pages
    n_phys = n_used + n_used // 4
    page_table = jax.random.permutation(next(ks), n_phys)[:n_used].reshape(DEC_BATCH, n_pages).astype(jnp.int32)
    return {
        'x_prompt': nrm((BATCH, SEQ, D_MODEL), 1.0),
        'x_sample': nrm((DEC_BATCH, DEC_SEQ, D_MODEL), 1.0),
        'p_prompt': nrm((DEPTH, BATCH, SEQ, D_PLE), 1.0),
        'p_sample': nrm((DEPTH, DEC_BATCH, DEC_SEQ, D_PLE), 1.0),
        'cache_ckv': nrm((DEPTH, n_phys, PAGE_SIZE, KV_RANK), 1.0),
        'cache_kpe': nrm((DEPTH, n_phys, PAGE_SIZE, ROPE_DIM), 1.0),
        'state_conv': nrm((DEPTH, DEC_BATCH, CONV_W - 1, W_B), 1.0),
        'page_table': page_table,
        'g_attn': gain((DEPTH, D_MODEL)),
        'w_in': nrm((DEPTH, D_MODEL, N_IN), D_MODEL ** -0.5),
        'g_q': gain((DEPTH, Q_RANK)),
        'g_kv': gain((DEPTH, KV_RANK)),
        'w_uq': nrm((DEPTH, Q_RANK, N_HEADS_A, NOPE_DIM + ROPE_DIM), Q_RANK ** -0.5),
        'w_uk': nrm((DEPTH, KV_RANK, N_HEADS_A, NOPE_DIM), KV_RANK ** -0.5),
        'w_uv': nrm((DEPTH, KV_RANK, N_HEADS_A, V_DIM), KV_RANK ** -0.5),
        'w_conv': nrm((DEPTH, CONV_W, W_B), CONV_W ** -0.5),
        'g_v': gain((DEPTH, W_C)),
        'b_v': nrm((DEPTH, W_C), 0.01),
        'w_s': nrm((DEPTH, N_GROUPS_C, CHUNK, CHUNK), CHUNK ** -0.5),
        'b_s': 1.0 + nrm((DEPTH, N_GROUPS_C, CHUNK), 0.01),
        'w_br_a': nrm((DEPTH, N_HEADS_A * V_DIM, D_MODEL), (N_HEADS_A * V_DIM) ** -0.5),
        'w_br_b': nrm((DEPTH, W_B, D_MODEL), W_B ** -0.5),
        'w_br_c': nrm((DEPTH, W_C, D_MODEL), W_C ** -0.5),
        'w_out': nrm((DEPTH, D_MODEL, D_MODEL), D_MODEL ** -0.5),
        'g_ffn': gain((DEPTH, D_MODEL)),
        'w_pq': nrm((DEPTH, D_MODEL, N_HEADS_P, 2, D_KEY // 2), D_MODEL ** -0.5),
        'sub_keys': nrm((DEPTH, N_HEADS_P, 2, N_KEYS, D_KEY // 2), (D_KEY // 2) ** -0.5),
        'u_tab': nrm((DEPTH, N_EXPERTS, D_MODEL), D_MODEL ** -0.5),
        'v_tab': nrm((DEPTH, N_EXPERTS, D_MODEL), 0.1),
        'g_ple': gain((DEPTH, D_MODEL)),
        'w_pg': nrm((DEPTH, D_MODEL, D_MODEL), D_MODEL ** -0.5),
        'w_ple': nrm((DEPTH, D_PLE, D_MODEL), D_PLE ** -0.5),
        'g_final': gain((D_MODEL,)),
    }


def reference(x_prompt, x_sample, p_prompt, p_sample, cache_ckv, cache_kpe, state_conv, page_table,
              g_attn, w_in, g_q, g_kv, w_uq, w_uk, w_uv, w_conv, g_v, b_v, w_s, b_s,
              w_br_a, w_br_b, w_br_c, w_out, g_ffn, w_pq, sub_keys, u_tab, v_tab,
              g_ple, w_pg, w_ple, g_final):
    dec_b = x_sample.shape[0]
    past_len = page_table.shape[1] * cache_ckv.shape[2]
    pos_prompt = jnp.arange(x_prompt.shape[1], dtype=jnp.int32)
    pos_sample = past_len + jnp.arange(x_sample.shape[1], dtype=jnp.int32)
    conv_zero = jnp.zeros((x_prompt.shape[0], CONV_W - 1, W_B), x_prompt.dtype)
    xp, xs = x_prompt, x_sample
    ckv_p_l, kpe_p_l, ckv_s_l, kpe_s_l, conv_p_l, conv_s_l, v_s_l = [], [], [], [], [], [], []
    for i in range(DEPTH):
        lw = (g_attn[i], w_in[i], g_q[i], g_kv[i], w_uq[i], w_uk[i], w_uv[i], w_conv[i],
              g_v[i], b_v[i], w_s[i], b_s[i], w_br_a[i], w_br_b[i], w_br_c[i], w_out[i],
              g_ffn[i], w_pq[i], sub_keys[i], u_tab[i], v_tab[i], g_ple[i], w_pg[i], w_ple[i])
        xp, ckv_p, kpe_p, conv_p, _ = _layer(xp, p_prompt[i], pos_prompt, conv_zero, _mla_prompt, *lw)
        past_c = cache_ckv[i][page_table].reshape(dec_b, past_len, KV_RANK)
        past_p = cache_kpe[i][page_table].reshape(dec_b, past_len, ROPE_DIM)
        attend_s = functools.partial(_mla_sample, past_ckv=past_c, past_kpe=past_p)
        xs, ckv_s, kpe_s, conv_s, v_s = _layer(xs, p_sample[i], pos_sample, state_conv[i], attend_s, *lw)
        ckv_p_l.append(ckv_p)
        kpe_p_l.append(kpe_p)
        conv_p_l.append(conv_p)
        ckv_s_l.append(ckv_s)
        kpe_s_l.append(kpe_s)
        conv_s_l.append(conv_s)
        v_s_l.append(v_s)
    y_prompt = _rms(xp, g_final)
    y_sample = _rms(xs, g_final)
    new_ckv_prompt = jnp.stack(ckv_p_l)
    new_kpe_prompt = jnp.stack(kpe_p_l)
    new_ckv_sample = jnp.stack(ckv_s_l)
    new_kpe_sample = jnp.stack(kpe_s_l)
    new_conv_prompt = jnp.stack(conv_p_l)
    new_conv_sample = jnp.stack(conv_s_l)
    new_chunk_v_sample = jnp.stack(v_s_l)
    return (y_prompt, y_sample, new_ckv_prompt, new_kpe_prompt, new_ckv_sample, new_kpe_sample,
            new_conv_prompt, new_conv_sample, new_chunk_v_sample)
```

```python
import functools

import jax
import jax.numpy as jnp
from jax import lax
from jax.experimental import pallas as pl
from jax.experimental.pallas import tpu as pltpu

F32 = jnp.float32
BF16 = jnp.bfloat16

N_HEADS_A = 8
NOPE_DIM = 64
ROPE_DIM = 32
V_DIM = 64
Q_RANK = 768
KV_RANK = 256
ROPE_THETA = 10000.0
Q_BLOCK = 128
MLA_SCALE = (NOPE_DIM + ROPE_DIM) ** -0.5
W_B = 512
CONV_W = 3
W_C = 512
N_GROUPS_C = 4
CHUNK = 128
N_KEYS = 128
N_HEADS_P = 8
TOPK_P = 16
EPS = 1e-6

NEG_BIG = -1e30
VMEM_LIMIT_BYTES = 56 * 1024 * 1024

_OFF_CQ = 0
_OFF_CKV = _OFF_CQ + Q_RANK
_OFF_CB = _OFF_CKV + KV_RANK
_OFF_CC = _OFF_CB + W_B
_OFF_CH = _OFF_CC + W_B
_OFF_U = _OFF_CH + W_B
_OFF_V = _OFF_U + W_C
_OFF_KPE = _OFF_V + W_C
_OFF_KPE_SW = _OFF_KPE + ROPE_DIM
_N_SMALL = _OFF_KPE_SW + ROPE_DIM


def _cparams(semantics):
    return pltpu.CompilerParams(dimension_semantics=semantics, vmem_limit_bytes=VMEM_LIMIT_BYTES)


def _rms_rows(x, g):
    return x * lax.rsqrt(jnp.mean(x * x, axis=-1, keepdims=True) + EPS) * g


def _dot(a, b):
    return jnp.dot(a, b, preferred_element_type=F32)


def _dot_nt(a, b):
    return lax.dot_general(a, b, (((1,), (1,)), ((), ())), preferred_element_type=F32)


def _const_spec(shape):
    nd = len(shape)
    return pl.BlockSpec(shape, lambda *_: (0,) * nd)


def _inproj_body(sample, tm, blocks_per_seq, d_model, *refs):
    if sample:
        (x_ref, gattn_ref, wsm_ref, wg_ref, gq_ref, gkv_ref, wuq_ref, wukbd_ref, cosk_ref, sink_ref,
         cosq_ref, sinq_ref, wconv_ref, gv_ref, bv_ref, ms_ref, bs_ref, f1_ref, f2_ref,
         qlat_ref, qpe_ref, ckv_ref, kpe_ref, kcb_ref, kpb_ref, bconv_ref, xc_ref, cmix_ref, vn_ref,
         sga_ref, sgb_ref, sgc_ref) = refs
        carry_ref = None
    else:
        (x_ref, gattn_ref, wsm_ref, wg_ref, gq_ref, gkv_ref, wuq_ref, wukbd_ref, cosk_ref, sink_ref,
         cosq_ref, sinq_ref, wconv_ref, gv_ref, bv_ref, ms_ref, bs_ref,
         qlat_ref, qpe_ref, ckv_ref, kpe_ref, kcb_ref, kpb_ref, bconv_ref, xc_ref, cmix_ref, vn_ref,
         sga_ref, sgb_ref, sgc_ref, carry_ref) = refs

    x = x_ref[...]
    h = _rms_rows(x, gattn_ref[...]).astype(BF16)

    def proj(a, b):
        return _dot(h, wsm_ref[:, a:b])

    cq = proj(_OFF_CQ, _OFF_CKV)
    cqn = _rms_rows(cq, gq_ref[...]).astype(BF16)
    qall = _dot(cqn, wuq_ref[...])
    n_nope = N_HEADS_A * NOPE_DIM
    n_pe = N_HEADS_A * ROPE_DIM
    qpe = (qall[:, n_nope:n_nope + n_pe] * cosq_ref[...]
           + qall[:, n_nope + n_pe:n_nope + 2 * n_pe] * sinq_ref[...]) * MLA_SCALE
    qlat = _dot(qall[:, :n_nope].astype(BF16), wukbd_ref[...]) * MLA_SCALE
    for hd in range(N_HEADS_A):
        qlat_ref[hd] = qlat[:, hd * KV_RANK:(hd + 1) * KV_RANK]
        qpe_ref[hd] = qpe[:, hd * ROPE_DIM:(hd + 1) * ROPE_DIM]

    ckv = _rms_rows(proj(_OFF_CKV, _OFF_CB), gkv_ref[...])
    ckv_ref[...] = ckv
    kcb_ref[...] = ckv.astype(BF16)
    kpe = proj(_OFF_KPE, _OFF_KPE_SW) * cosk_ref[...] + proj(_OFF_KPE_SW, _N_SMALL) * sink_ref[...]
    kpe_ref[...] = kpe
    kpb_ref[...] = kpe.astype(BF16)

    cb = proj(_OFF_CB, _OFF_CC)
    xc = proj(_OFF_CC, _OFF_CH) * proj(_OFF_CH, _OFF_U)
    xc_ref[...] = xc
    rows = lax.broadcasted_iota(jnp.int32, (tm, 1), 0)
    r1 = pltpu.roll(xc, 1, 0)
    r2 = pltpu.roll(xc, 2, 0)
    if sample:
        pos = rows % 8
        m1 = jnp.where(pos == 0, f1_ref[...], r1)
        m2 = jnp.where(pos < 2, f2_ref[...], r2)
    else:
        first = (pl.program_id(0) % blocks_per_seq) == 0
        prev = jnp.where(first, 0.0, carry_ref[...])
        p6 = prev[6:7, :]
        p7 = prev[7:8, :]
        m1 = jnp.where(rows == 0, p7, r1)
        m2 = jnp.where(rows == 0, p6, jnp.where(rows == 1, p7, r2))
        carry_ref[...] = xc[tm - 8:tm, :]
    y = m2 * wconv_ref[0:1, :] + m1 * wconv_ref[1:2, :] + xc * wconv_ref[2:3, :]
    bconv_ref[...] = (cb * y).astype(BF16)

    u = proj(_OFF_U, _OFF_V)
    v = proj(_OFF_V, _OFF_KPE)
    vc = v - jnp.mean(v, axis=-1, keepdims=True)
    vn = vc * lax.rsqrt(jnp.mean(vc * vc, axis=-1, keepdims=True) + EPS) * gv_ref[...] + bv_ref[...]
    vn_ref[...] = vn
    vnb = vn.astype(BF16)
    gw = W_C // N_GROUPS_C
    for r0 in range(0, tm, CHUNK):
        for g in range(N_GROUPS_C):
            mix = _dot(ms_ref[g], vnb[r0:r0 + CHUNK, g * gw:(g + 1) * gw]) + bs_ref[:, g * gw:(g + 1) * gw]
            cmix_ref[r0:r0 + CHUNK, g * gw:(g + 1) * gw] = (
                u[r0:r0 + CHUNK, g * gw:(g + 1) * gw] * mix).astype(BF16)

    sga_ref[...] = jax.nn.sigmoid(_dot(h, wg_ref[:, 0:d_model]))
    sgb_ref[...] = jax.nn.sigmoid(_dot(h, wg_ref[:, d_model:2 * d_model]))
    sgc_ref[...] = jax.nn.sigmoid(_dot(h, wg_ref[:, 2 * d_model:3 * d_model]))


def _inproj(x, lw, tabs, sample, seq_len, conv_fix, tm):
    t, d = x.shape
    nblk = t // tm
    blocks_per_seq = max(seq_len // tm, 1)
    row = lambda w: pl.BlockSpec((tm, w), lambda i: (i, 0))
    if sample:
        tab_spec = lambda w: pl.BlockSpec((tm, w), lambda i: (0, 0))
    else:
        tab_spec = lambda w: pl.BlockSpec((tm, w), lambda i: (i % blocks_per_seq, 0))
    n_pe = N_HEADS_A * ROPE_DIM
    in_specs = [
        row(d), _const_spec((1, d)), _const_spec(lw["w_small"].shape), _const_spec(lw["w_gates"].shape),
        _const_spec((1, Q_RANK)), _const_spec((1, KV_RANK)), _const_spec(lw["w_uq"].shape),
        _const_spec(lw["w_uk_bd"].shape),
        tab_spec(ROPE_DIM), tab_spec(ROPE_DIM), tab_spec(n_pe), tab_spec(n_pe),
        _const_spec((CONV_W, W_B)), _const_spec((1, W_C)), _const_spec((1, W_C)),
        _const_spec((N_GROUPS_C, CHUNK, CHUNK)), _const_spec((CHUNK, W_C)),
    ]
    args = [x, lw["g_attn"], lw["w_small"], lw["w_gates"], lw["g_q"], lw["g_kv"], lw["w_uq"], lw["w_uk_bd"],
            tabs["cosk"], tabs["sink"], tabs["cosq"], tabs["sinq"], lw["w_conv"], lw["g_v"], lw["b_v"],
            lw["ms_s"] if sample else lw["ms_p"], lw["bs_s"] if sample else lw["bs_p"]]
    scratch = []
    if sample:
        in_specs += [row(W_B), row(W_B)]
        args += [conv_fix[0], conv_fix[1]]
    else:
        scratch = [pltpu.VMEM((8, W_B), F32)]
    out_shape = [
        jax.ShapeDtypeStruct((N_HEADS_A, t, KV_RANK), F32),
        jax.ShapeDtypeStruct((N_HEADS_A, t, ROPE_DIM), F32),
        jax.ShapeDtypeStruct((t, KV_RANK), F32),
        jax.ShapeDtypeStruct((t, ROPE_DIM), F32),
        jax.ShapeDtypeStruct((t, KV_RANK), BF16),
        jax.ShapeDtypeStruct((t, ROPE_DIM), BF16),
        jax.ShapeDtypeStruct((t, W_B), BF16),
        jax.ShapeDtypeStruct((t, W_B), F32),
        jax.ShapeDtypeStruct((t, W_C), BF16),
        jax.ShapeDtypeStruct((t, W_C), F32),
        jax.ShapeDtypeStruct((t, d), F32),
        jax.ShapeDtypeStruct((t, d), F32),
        jax.ShapeDtypeStruct((t, d), F32),
    ]
    head_spec = lambda w: pl.BlockSpec((N_HEADS_A, tm, w), lambda i: (0, i, 0))
    out_specs = [head_spec(KV_RANK), head_spec(ROPE_DIM), row(KV_RANK), row(ROPE_DIM), row(KV_RANK),
                 row(ROPE_DIM), row(W_B), row(W_B), row(W_C), row(W_C), row(d), row(d), row(d)]
    return pl.pallas_call(
        functools.partial(_inproj_body, sample, tm, blocks_per_seq, d),
        grid=(nblk,),
        in_specs=in_specs,
        out_specs=out_specs,
        out_shape=out_shape,
        scratch_shapes=scratch,
        compiler_params=_cparams(("arbitrary",)),
        name="inproj_sample" if sample else "inproj_prompt",
    )(*args)


def _softmax_step(s, k_bf, m_ref, l_ref, acc_ref):
    m_prev = m_ref[...]
    m_next = jnp.maximum(m_prev, jnp.max(s, axis=1, keepdims=True))
    p = jnp.exp(s - m_next)
    alpha = jnp.exp(m_prev - m_next)
    l_ref[...] = l_ref[...] * alpha + jnp.sum(p, axis=1, keepdims=True)
    acc_ref[...] = acc_ref[...] * alpha + _dot(p.astype(BF16), k_bf)
    m_ref[...] = m_next


def _attn_prompt_body(kb, qlat_ref, qpe_ref, kc_ref, kp_ref, o_ref, m_ref, l_ref, acc_ref):
    qi = pl.program_id(1)
    kj = pl.program_id(2)
    rows = N_HEADS_A * Q_BLOCK
    last = ((qi + 1) * Q_BLOCK - 1) // kb

    @pl.when(kj == 0)
    def _():
        m_ref[...] = jnp.full((rows, 1), NEG_BIG, F32)
        l_ref[...] = jnp.zeros((rows, 1), F32)
        acc_ref[...] = jnp.zeros((rows, KV_RANK), F32)

    @pl.when(kj <= last)
    def _():
        q = qlat_ref[...].reshape(rows, KV_RANK).astype(BF16)
        qp = qpe_ref[...].reshape(rows, ROPE_DIM).astype(BF16)
        k = kc_ref[...]
        s = _dot_nt(q, k) + _dot_nt(qp, kp_ref[...])
        qpos = qi * Q_BLOCK + lax.broadcasted_iota(jnp.int32, (rows, 1), 0) % Q_BLOCK
        kpos = kj * kb + lax.broadcasted_iota(jnp.int32, (1, kb), 1)
        s = jnp.where(kpos <= qpos, s, NEG_BIG)
        _softmax_step(s, k, m_ref, l_ref, acc_ref)

    @pl.when(kj == last)
    def _():
        o = acc_ref[...] / l_ref[...]
        o_ref[...] = o.reshape(N_HEADS_A, Q_BLOCK, KV_RANK).astype(BF16)


def _attn_prompt(qlat, qpe, kcb, kpb, batch, seq, kb):
    nq = seq // Q_BLOCK
    nk = seq // kb
    t = batch * seq

    def kmap(b, qi, kj):
        return (b * nk + jnp.minimum(kj, ((qi + 1) * Q_BLOCK - 1) // kb), 0)

    qmap = lambda b, qi, kj: (0, b * nq + qi, 0)
    rows = N_HEADS_A * Q_BLOCK
    return pl.pallas_call(
        functools.partial(_attn_prompt_body, kb),
        grid=(batch, nq, nk),
        in_specs=[
            pl.BlockSpec((N_HEADS_A, Q_BLOCK, KV_RANK), qmap),
            pl.BlockSpec((N_HEADS_A, Q_BLOCK, ROPE_DIM), qmap),
            pl.BlockSpec((kb, KV_RANK), kmap),
            pl.BlockSpec((kb, ROPE_DIM), kmap),
        ],
        out_specs=pl.BlockSpec((N_HEADS_A, Q_BLOCK, KV_RANK), qmap),
        out_shape=jax.ShapeDtypeStruct((N_HEADS_A, t, KV_RANK), BF16),
        scratch_shapes=[pltpu.VMEM((rows, 1), F32), pltpu.VMEM((rows, 1), F32),
                        pltpu.VMEM((rows, KV_RANK), F32)],
        compiler_params=_cparams(("arbitrary", "arbitrary", "arbitrary")),
        name="attn_prompt",
    )(qlat, qpe, kcb, kpb)


def _attn_sample_body(pp, dec_seq, page, *refs):
    pt_ref = refs[0]
    del pt_ref
    qlat_ref, qpe_ref = refs[1], refs[2]
    c_refs = refs[3:3 + pp]
    p_refs = refs[3 + pp:3 + 2 * pp]
    knc_ref, knp_ref = refs[3 + 2 * pp], refs[4 + 2 * pp]
    o_ref = refs[5 + 2 * pp]
    kbuf_ref, pbuf_ref, m_ref, l_ref, acc_ref = refs[6 + 2 * pp:]
    j = pl.program_id(1)
    nj = pl.num_programs(1)
    rows = N_HEADS_A * dec_seq

    @pl.when(j == 0)
    def _():
        m_ref[...] = jnp.full((rows, 1), NEG_BIG, F32)
        l_ref[...] = jnp.zeros((rows, 1), F32)
        acc_ref[...] = jnp.zeros((rows, KV_RANK), F32)

    q = qlat_ref[...].reshape(rows, KV_RANK).astype(BF16)
    qp = qpe_ref[...].reshape(rows, ROPE_DIM).astype(BF16)
    for p in range(pp):
        kbuf_ref[p * page:(p + 1) * page, :] = c_refs[p][...].astype(BF16)
        pbuf_ref[p * page:(p + 1) * page, :] = p_refs[p][...].astype(BF16)
    k = kbuf_ref[...]
    s = _dot_nt(q, k) + _dot_nt(qp, pbuf_ref[...])
    _softmax_step(s, k, m_ref, l_ref, acc_ref)

    @pl.when(j == nj - 1)
    def _():
        kn = knc_ref[...]
        sn = _dot_nt(q, kn) + _dot_nt(qp, knp_ref[...])
        qpos = lax.broadcasted_iota(jnp.int32, (rows, 1), 0) % dec_seq
        kpos = lax.broadcasted_iota(jnp.int32, (1, page), 1)
        sn = jnp.where(kpos <= qpos, sn, NEG_BIG)
        _softmax_step(sn, kn, m_ref, l_ref, acc_ref)
        o = acc_ref[...] / l_ref[...]
        o_ref[...] = o.reshape(N_HEADS_A, dec_seq, KV_RANK)


def _attn_sample(layer, qlat, qpe, cache_ckv, cache_kpe, page_table, knew_c, knew_p, dec_seq, pp):
    dec_b, n_pages = page_table.shape
    page = cache_ckv.shape[2]
    t = dec_b * dec_seq
    qmap = lambda b, j, pt: (0, b, 0)

    def cmap(p):
        return lambda b, j, pt: (layer, pt[b, j * pp + p], 0, 0)

    in_specs = [pl.BlockSpec((N_HEADS_A, dec_seq, KV_RANK), qmap),
                pl.BlockSpec((N_HEADS_A, dec_seq, ROPE_DIM), qmap)]
    in_specs += [pl.BlockSpec((None, None, page, KV_RANK), cmap(p)) for p in range(pp)]
    in_specs += [pl.BlockSpec((None, None, page, ROPE_DIM), cmap(p)) for p in range(pp)]
    in_specs += [pl.BlockSpec((None, page, KV_RANK), lambda b, j, pt: (b, 0, 0)),
                 pl.BlockSpec((None, page, ROPE_DIM), lambda b, j, pt: (b, 0, 0))]
    rows = N_HEADS_A * dec_seq
    grid_spec = pltpu.PrefetchScalarGridSpec(
        num_scalar_prefetch=1,
        grid=(dec_b, n_pages // pp),
        in_specs=in_specs,
        out_specs=pl.BlockSpec((N_HEADS_A, dec_seq, KV_RANK), qmap),
        scratch_shapes=[pltpu.VMEM((pp * page, KV_RANK), BF16), pltpu.VMEM((pp * page, ROPE_DIM), BF16),
                        pltpu.VMEM((rows, 1), F32), pltpu.VMEM((rows, 1), F32),
                        pltpu.VMEM((rows, KV_RANK), F32)],
    )
    return pl.pallas_call(
        functools.partial(_attn_sample_body, pp, dec_seq, page),
        grid_spec=grid_spec,
        out_shape=jax.ShapeDtypeStruct((N_HEADS_A, t, KV_RANK), F32),
        compiler_params=_cparams(("arbitrary", "arbitrary")),
        name="attn_sample",
    )(page_table, qlat, qpe, *([cache_ckv] * pp), *([cache_kpe] * pp), knew_c, knew_p)


def _merge_body(x_ref, olat_ref, wuvbd_ref, bconv_ref, cmix_ref, sga_ref, sgb_ref, sgc_ref,
                wbra_ref, wbrb_ref, wbrc_ref, wout_ref, gffn_ref, wpq_ref, skeys_ref,
                x1_ref, h2t_ref, st_ref):
    ocat = jnp.concatenate([olat_ref[hd].astype(BF16) for hd in range(N_HEADS_A)], axis=1)
    a = _dot(ocat, wuvbd_ref[...]).astype(BF16)
    merged = (sga_ref[...] * _dot(a, wbra_ref[...])
              + sgb_ref[...] * _dot(bconv_ref[...], wbrb_ref[...])
              + sgc_ref[...] * _dot(cmix_ref[...], wbrc_ref[...]))
    x1 = x_ref[...] + _dot(merged.astype(BF16), wout_ref[...])
    x1_ref[...] = x1
    h2 = _rms_rows(x1, gffn_ref[...])
    h2t_ref[...] = h2.T.astype(BF16)
    q = _dot(h2.astype(BF16), wpq_ref[...]).astype(BF16)
    half = q.shape[1] // (2 * N_HEADS_P)
    for hc in range(2 * N_HEADS_P):
        st_ref[hc * N_KEYS:(hc + 1) * N_KEYS, :] = _dot_nt(skeys_ref[hc], q[:, hc * half:(hc + 1) * half])


def _merge(x, olat, bconv, cmix, sga, sgb, sgc, lw, tm):
    t, d = x.shape
    row = lambda w: pl.BlockSpec((tm, w), lambda i: (i, 0))
    col = lambda r: pl.BlockSpec((r, tm), lambda i: (0, i))
    n_s = 2 * N_HEADS_P * N_KEYS
    return pl.pallas_call(
        _merge_body,
        grid=(t // tm,),
        in_specs=[row(d), pl.BlockSpec((N_HEADS_A, tm, KV_RANK), lambda i: (0, i, 0)),
                  _const_spec(lw["w_uv_bd"].shape), row(W_B), row(W_C), row(d), row(d), row(d),
                  _const_spec(lw["w_br_a"].shape), _const_spec(lw["w_br_b"].shape),
                  _const_spec(lw["w_br_c"].shape), _const_spec(lw["w_out"].shape), _const_spec((1, d)),
                  _const_spec(lw["w_pq"].shape), _const_spec(lw["skeys"].shape)],
        out_specs=[row(d), col(d), col(n_s)],
        out_shape=[jax.ShapeDtypeStruct((t, d), F32), jax.ShapeDtypeStruct((d, t), BF16),
                   jax.ShapeDtypeStruct((n_s, t), F32)],
        compiler_params=_cparams(("arbitrary",)),
        name="merge",
    )(x, olat, lw["w_uv_bd"], bconv, cmix, sga, sgb, sgc, lw["w_br_a"], lw["w_br_b"], lw["w_br_c"],
      lw["w_out"], lw["g_ffn"], lw["w_pq"], lw["skeys"])


def _top_values(cur, n, out_ref, base):
    for k in range(n):
        mx = jnp.max(cur, axis=0, keepdims=True)
        out_ref[base + k:base + k + 1, :] = mx
        if k + 1 < n:
            cur = jnp.where(cur == mx, -jnp.inf, cur)


def _peer_select_body(st_ref, a0_ref, b1_ref, thr_ref, top_ref, cand_ref):
    tn = st_ref.shape[1]
    for hc in range(2 * N_HEADS_P):
        _top_values(st_ref[hc * N_KEYS:(hc + 1) * N_KEYS, :], TOPK_P, top_ref, hc * TOPK_P)
    rank = lax.broadcasted_iota(jnp.int32, (TOPK_P, 1), 0)
    for hd in range(N_HEADS_P):
        s0 = top_ref[(2 * hd) * TOPK_P:(2 * hd + 1) * TOPK_P, :]
        s1 = top_ref[(2 * hd + 1) * TOPK_P:(2 * hd + 2) * TOPK_P, :]
        parts = []
        for b in range(TOPK_P):
            c = s0 + s1[b:b + 1, :]
            parts.append(jnp.where(rank < TOPK_P // (b + 1), c, -jnp.inf))
        cands = jnp.concatenate(parts, axis=0)
        _top_values(cands, TOPK_P, cand_ref, 0)
        thr = cand_ref[TOPK_P - 1:TOPK_P, :]
        mx = cand_ref[0:1, :]
        z = jnp.sum(jnp.where(cands >= thr, jnp.exp(cands - mx), 0.0), axis=0, keepdims=True)
        thr_ref[hd:hd + 1, :] = thr
        r0 = 2 * hd * N_KEYS
        a0_ref[hd * N_KEYS:(hd + 1) * N_KEYS, :] = jnp.exp(st_ref[r0:r0 + N_KEYS, :] - s0[0:1, :])
        b1_ref[hd * N_KEYS:(hd + 1) * N_KEYS, :] = (
            jnp.exp(st_ref[r0 + N_KEYS:r0 + 2 * N_KEYS, :] - s1[0:1, :]) / z)
    del tn


def _peer_select(st, tn):
    n_s, t = st.shape
    n_h = N_HEADS_P * N_KEYS
    col = lambda r: pl.BlockSpec((r, tn), lambda i: (0, i))
    return pl.pallas_call(
        _peer_select_body,
        grid=(t // tn,),
        in_specs=[col(n_s)],
        out_specs=[col(n_h), col(n_h), col(N_HEADS_P)],
        out_shape=[jax.ShapeDtypeStruct((n_h, t), F32), jax.ShapeDtypeStruct((n_h, t), F32),
                   jax.ShapeDtypeStruct((N_HEADS_P, t), F32)],
        scratch_shapes=[pltpu.VMEM((2 * N_HEADS_P * TOPK_P, tn), F32), pltpu.VMEM((TOPK_P, tn), F32)],
        compiler_params=_cparams(("arbitrary",)),
        name="peer_select",
    )(st)


def _peer_dense_body(eb, lane_chunk, h2t_ref, u_ref, vt_ref, st_ref, a0_ref, b1_ref, thr_ref,
                     out_ref, w_ref):
    e = pl.program_id(1)
    tn = h2t_ref.shape[1]
    keys_per_block = eb // N_KEYS

    @pl.when(e == 0)
    def _():
        out_ref[...] = jnp.zeros(out_ref.shape, F32)

    at = _dot(u_ref[...], h2t_ref[...])
    i_base = pl.multiple_of(e * keys_per_block, keys_per_block)
    for ii in range(keys_per_block):
        for c0 in range(0, tn, lane_chunk):
            lanes = slice(c0, c0 + lane_chunk)
            xa = at[ii * N_KEYS:(ii + 1) * N_KEYS, lanes]
            act = 0.5 * xa * (1.0 + lax.erf(xa * (2.0 ** -0.5)))
            gate = jnp.zeros((N_KEYS, lane_chunk), F32)
            for hd in range(N_HEADS_P):
                s0 = st_ref[pl.ds(2 * hd * N_KEYS + i_base, keys_per_block), lanes][ii:ii + 1, :]
                a0 = a0_ref[pl.ds(hd * N_KEYS + i_base, keys_per_block), lanes][ii:ii + 1, :]
                s1 = st_ref[(2 * hd + 1) * N_KEYS:(2 * hd + 2) * N_KEYS, lanes]
                b1 = b1_ref[hd * N_KEYS:(hd + 1) * N_KEYS, lanes]
                gate = gate + jnp.where((s0 + s1) >= thr_ref[hd:hd + 1, lanes], a0 * b1, 0.0)
            w_ref[ii * N_KEYS:(ii + 1) * N_KEYS, lanes] = (gate * act).astype(BF16)
    out_ref[...] += _dot(vt_ref[...], w_ref[...])


def _peer_dense(h2t, u_bf, vt_bf, st, a0, b1, thr, tn, eb):
    d, t = h2t.shape
    n_e = u_bf.shape[0]
    assert eb == 8 * N_KEYS
    col = lambda r: pl.BlockSpec((r, tn), lambda i, e: (0, i))
    return pl.pallas_call(
        functools.partial(_peer_dense_body, eb, 128),
        grid=(t // tn, n_e // eb),
        in_specs=[col(d), pl.BlockSpec((eb, d), lambda i, e: (e, 0)), pl.BlockSpec((d, eb), lambda i, e: (0, e)),
                  col(st.shape[0]), col(a0.shape[0]), col(b1.shape[0]), col(thr.shape[0])],
        out_specs=col(d),
        out_shape=jax.ShapeDtypeStruct((d, t), F32),
        scratch_shapes=[pltpu.VMEM((eb, tn), BF16)],
        compiler_params=_cparams(("arbitrary", "arbitrary")),
        name="peer_dense",
    )(h2t, u_bf, vt_bf, st, a0, b1, thr)


def _ple_body(last, x1_ref, pt_ref, gple_ref, wpg_ref, p_ref, wple_ref, *rest):
    if last:
        gfin_ref, x3_ref, y_ref = rest
    else:
        (x3_ref,) = rest
    x2 = x1_ref[...] + pt_ref[...].T
    hp = _rms_rows(x2, gple_ref[...]).astype(BF16)
    gate = jax.nn.sigmoid(_dot(hp, wpg_ref[...]))
    x3 = x2 + gate * _dot(p_ref[...].astype(BF16), wple_ref[...])
    x3_ref[...] = x3
    if last:
        y_ref[...] = _rms_rows(x3, gfin_ref[...])


def _ple(x1, peer_t, col_off, p_i, lw, g_final, last, tm):
    t, d = x1.shape
    row = lambda w: pl.BlockSpec((tm, w), lambda i: (i, 0))
    in_specs = [row(d), pl.BlockSpec((d, tm), lambda i: (0, i + col_off)), _const_spec((1, d)),
                _const_spec(lw["w_pg"].shape), row(p_i.shape[1]), _const_spec(lw["w_ple"].shape)]
    args = [x1, peer_t, lw["g_ple"], lw["w_pg"], p_i, lw["w_ple"]]
    out_specs = [row(d)]
    out_shape = [jax.ShapeDtypeStruct((t, d), F32)]
    if last:
        in_specs.append(_const_spec((1, d)))
        args.append(g_final)
        out_specs.append(row(d))
        out_shape.append(jax.ShapeDtypeStruct((t, d), F32))
    return pl.pallas_call(
        functools.partial(_ple_body, last),
        grid=(t // tm,),
        in_specs=in_specs,
        out_specs=out_specs,
        out_shape=out_shape,
        compiler_params=_cparams(("arbitrary",)),
        name="ple_last" if last else "ple",
    )(*args)


def _rope_tables(pos):
    inv = 1.0 / (ROPE_THETA ** (jnp.arange(0, ROPE_DIM, 2, dtype=F32) / ROPE_DIM))
    ang = pos.astype(F32)[:, None] * inv[None, :]
    cos, sin = jnp.cos(ang), jnp.sin(ang)
    cosk = jnp.concatenate([cos, cos], axis=1)
    sink = jnp.concatenate([-sin, sin], axis=1)
    return {"cosk": cosk, "sink": sink, "cosq": jnp.tile(cosk, (1, N_HEADS_A)),
            "sinq": jnp.tile(sink, (1, N_HEADS_A))}


def _swap_halves(w):
    half = w.shape[-1] // 2
    return jnp.concatenate([w[..., half:], w[..., :half]], axis=-1)


def _block_diag(blocks):
    n, r, c = blocks.shape
    eye = jnp.eye(n, dtype=blocks.dtype)
    return (eye[:, None, :, None] * blocks[:, :, None, :]).reshape(n * r, n * c)


def _layer_weights(i, d, dec_seq, g_attn, w_in, g_q, g_kv, w_uq, w_uk, w_uv, w_conv, g_v, b_v, w_s, b_s,
                   w_br_a, w_br_b, w_br_c, w_out, g_ffn, w_pq, sub_keys, u_tab, v_tab, g_ple, w_pg, w_ple):
    bf = lambda a: a.astype(BF16)
    wi = w_in[i]
    o_kpe = Q_RANK + KV_RANK
    o_cb = o_kpe + ROPE_DIM
    o_g = o_cb + 3 * W_B + 2 * W_C
    w_kpe = wi[:, o_kpe:o_cb]
    w_small = jnp.concatenate([wi[:, :o_kpe], wi[:, o_cb:o_g], w_kpe, _swap_halves(w_kpe)], axis=1)
    uq = w_uq[i]
    uq_pe = uq[:, :, NOPE_DIM:]
    w_uq_p = jnp.concatenate([uq[:, :, :NOPE_DIM].reshape(Q_RANK, -1), uq_pe.reshape(Q_RANK, -1),
                              _swap_halves(uq_pe).reshape(Q_RANK, -1)], axis=1)
    ws = jnp.tril(w_s[i])
    ws_small = jnp.tril(w_s[i][:, :dec_seq, :dec_seq])
    eye = jnp.eye(CHUNK // dec_seq, dtype=F32)
    ms_s = jnp.einsum("ab,gts->gatbs", eye, ws_small).reshape(N_GROUPS_C, CHUNK, CHUNK)
    gw = W_C // N_GROUPS_C
    bs_p = jnp.repeat(b_s[i].T, gw, axis=1)
    bs_s = jnp.tile(bs_p[:dec_seq], (CHUNK // dec_seq, 1))
    return {
        "g_attn": g_attn[i][None], "w_small": bf(w_small), "w_gates": bf(wi[:, o_g:]),
        "g_q": g_q[i][None], "g_kv": g_kv[i][None], "w_uq": bf(w_uq_p),
        "w_uk_bd": bf(_block_diag(jnp.transpose(w_uk[i], (1, 2, 0)))),
        "w_uv_bd": bf(_block_diag(jnp.transpose(w_uv[i], (1, 0, 2)))),
        "w_conv": w_conv[i], "g_v": g_v[i][None], "b_v": b_v[i][None],
        "ms_p": bf(ws), "ms_s": bf(ms_s), "bs_p": bs_p, "bs_s": bs_s,
        "w_br_a": bf(w_br_a[i]), "w_br_b": bf(w_br_b[i]), "w_br_c": bf(w_br_c[i]), "w_out": bf(w_out[i]),
        "g_ffn": g_ffn[i][None], "w_pq": bf(w_pq[i].reshape(d, -1)),
        "skeys": bf(sub_keys[i].reshape(2 * N_HEADS_P, N_KEYS, -1)),
        "u_tab": bf(u_tab[i]), "vt_tab": bf(v_tab[i].T),
        "g_ple": g_ple[i][None], "w_pg": bf(w_pg[i]), "w_ple": bf(w_ple[i]),
    }


def _pick_tile(n, prefs):
    for p in prefs:
        if n % p == 0:
            return p
    raise ValueError(f"no tile for {n}")


def kernel(x_prompt, x_sample, p_prompt, p_sample, cache_ckv, cache_kpe, state_conv, page_table,
           g_attn, w_in, g_q, g_kv, w_uq, w_uk, w_uv, w_conv, g_v, b_v, w_s, b_s,
           w_br_a, w_br_b, w_br_c, w_out, g_ffn, w_pq, sub_keys, u_tab, v_tab,
           g_ple, w_pg, w_ple, g_final):
    batch, seq, d = x_prompt.shape
    dec_b, dec_seq, _ = x_sample.shape
    depth = w_in.shape[0]
    n_pages = page_table.shape[1]
    page = cache_ckv.shape[2]
    past_len = n_pages * page
    tp, ts = batch * seq, dec_b * dec_seq
    assert seq % CHUNK == 0 and CHUNK % dec_seq == 0 and ts % CHUNK == 0 and dec_seq == 8
    assert page == CHUNK

    tm_p = _pick_tile(seq, (256, 128))
    tm_s = _pick_tile(ts, (256, 128))
    kb = _pick_tile(seq, (256, 128))
    pp = _pick_tile(n_pages, (16, 8, 4, 2, 1))
    tn = _pick_tile(tp + ts, (512, 256, 128))
    eb = 1024

    tabs_p = _rope_tables(jnp.arange(seq, dtype=jnp.int32))
    tabs_s = _rope_tables(past_len + (jnp.arange(tm_s, dtype=jnp.int32) % dec_seq))
    g_fin = g_final[None]

    xp = x_prompt.reshape(tp, d)
    xs = x_sample.reshape(ts, d)
    outs = {k: [] for k in ("ckv_p", "kpe_p", "ckv_s", "kpe_s", "conv_p", "conv_s", "v_s")}
    y_p = y_s = None
    for i in range(depth):
        lw = _layer_weights(i, d, dec_seq, g_attn, w_in, g_q, g_kv, w_uq, w_uk, w_uv, w_conv, g_v, b_v,
                            w_s, b_s, w_br_a, w_br_b, w_br_c, w_out, g_ffn, w_pq, sub_keys, u_tab, v_tab,
                            g_ple, w_pg, w_ple)
        last = i == depth - 1

        (qlat_p, qpe_p, ckv_p, kpe_p, kcb_p, kpb_p, bconv_p, xc_p, cmix_p, _vn_p, sga_p, sgb_p, sgc_p) = _inproj(
            xp, lw, tabs_p, False, seq, None, tm_p)
        olat_p = _attn_prompt(qlat_p, qpe_p, kcb_p, kpb_p, batch, seq, kb)
        x1_p, h2t_p, st_p = _merge(xp, olat_p, bconv_p, cmix_p, sga_p, sgb_p, sgc_p, lw, tm_p)

        st_c = state_conv[i]
        zero_row = jnp.zeros((dec_b, 1, W_B), F32)
        f1 = jnp.concatenate([st_c[:, 1:2], jnp.zeros((dec_b, dec_seq - 1, W_B), F32)], axis=1).reshape(ts, W_B)
        f2 = jnp.concatenate([st_c[:, 0:1], st_c[:, 1:2], jnp.zeros((dec_b, dec_seq - 2, W_B), F32)],
                             axis=1).reshape(ts, W_B)
        del zero_row
        (qlat_s, qpe_s, ckv_s, kpe_s, kcb_s, kpb_s, bconv_s, xc_s, cmix_s, vn_s, sga_s, sgb_s, sgc_s) = _inproj(
            xs, lw, tabs_s, True, dec_seq, (f1, f2), tm_s)
        knew_c = jnp.pad(kcb_s.reshape(dec_b, dec_seq, KV_RANK), ((0, 0), (0, page - dec_seq), (0, 0)))
        knew_p = jnp.pad(kpb_s.reshape(dec_b, dec_seq, ROPE_DIM), ((0, 0), (0, page - dec_seq), (0, 0)))
        olat_s = _attn_sample(i, qlat_s, qpe_s, cache_ckv, cache_kpe, page_table, knew_c, knew_p, dec_seq, pp)
        x1_s, h2t_s, st_s = _merge(xs, olat_s, bconv_s, cmix_s, sga_s, sgb_s, sgc_s, lw, tm_s)

        h2t = jnp.concatenate([h2t_p, h2t_s], axis=1)
        st = jnp.concatenate([st_p, st_s], axis=1)
        a0, b1, thr = _peer_select(st, tn)
        peer_t = _peer_dense(h2t, lw["u_tab"], lw["vt_tab"], st, a0, b1, thr, tn, eb)

        res_p = _ple(x1_p, peer_t, 0, p_prompt[i].reshape(tp, -1), lw, g_fin, last, tm_p)
        res_s = _ple(x1_s, peer_t, tp // tm_s, p_sample[i].reshape(ts, -1), lw, g_fin, last, tm_s)
        xp, xs = res_p[0], res_s[0]
        if last:
            y_p, y_s = res_p[1], res_s[1]

        outs["ckv_p"].append(ckv_p.reshape(batch, seq, KV_RANK))
        outs["kpe_p"].append(kpe_p.reshape(batch, seq, ROPE_DIM))
        outs["ckv_s"].append(ckv_s.reshape(dec_b, dec_seq, KV_RANK))
        outs["kpe_s"].append(kpe_s.reshape(dec_b, dec_seq, ROPE_DIM))
        outs["conv_p"].append(xc_p.reshape(batch, seq, W_B)[:, seq - (CONV_W - 1):])
        outs["conv_s"].append(xc_s.reshape(dec_b, dec_seq, W_B)[:, dec_seq - (CONV_W - 1):])
        outs["v_s"].append(vn_s.reshape(dec_b, dec_seq, W_C))

    return (y_p.reshape(batch, seq, d), y_s.reshape(dec_b, dec_seq, d),
            jnp.stack(outs["ckv_p"]), jnp.stack(outs["kpe_p"]), jnp.stack(outs["ckv_s"]),
            jnp.stack(outs["kpe_s"]), jnp.stack(outs["conv_p"]), jnp.stack(outs["conv_s"]),
            jnp.stack(outs["v_s"]))
```

```python
import functools

import jax
import jax.numpy as jnp
from jax import lax
from jax.experimental import pallas as pl
from jax.experimental.pallas import tpu as pltpu

F32 = jnp.float32
BF16 = jnp.bfloat16

N_HEADS_A = 8
NOPE_DIM = 64
ROPE_DIM = 32
V_DIM = 64
Q_RANK = 768
KV_RANK = 256
ROPE_THETA = 10000.0
Q_BLOCK = 128
MLA_SCALE = (NOPE_DIM + ROPE_DIM) ** -0.5
W_B = 512
CONV_W = 3
W_C = 512
N_GROUPS_C = 4
CHUNK = 128
N_KEYS = 128
N_HEADS_P = 8
TOPK_P = 16
EPS = 1e-6

NEG_BIG = -1e30
VMEM_LIMIT_BYTES = 56 * 1024 * 1024

_OFF_CQ = 0
_OFF_CKV = _OFF_CQ + Q_RANK
_OFF_CB = _OFF_CKV + KV_RANK
_OFF_CC = _OFF_CB + W_B
_OFF_CH = _OFF_CC + W_B
_OFF_U = _OFF_CH + W_B
_OFF_V = _OFF_U + W_C
_OFF_KPE = _OFF_V + W_C
_OFF_KPE_SW = _OFF_KPE + ROPE_DIM
_N_SMALL = _OFF_KPE_SW + ROPE_DIM


def _cparams(semantics):
    return pltpu.CompilerParams(dimension_semantics=semantics, vmem_limit_bytes=VMEM_LIMIT_BYTES)


def _rms_rows(x, g):
    return x * lax.rsqrt(jnp.mean(x * x, axis=-1, keepdims=True) + EPS) * g


def _dot(a, b):
    return jnp.dot(a, b, preferred_element_type=F32)


def _dot_nt(a, b):
    return lax.dot_general(a, b, (((1,), (1,)), ((), ())), preferred_element_type=F32)


def _const_spec(shape):
    nd = len(shape)
    return pl.BlockSpec(shape, lambda *_: (0,) * nd)


def _inproj_body(sample, tm, blocks_per_seq, d_model, *refs):
    if sample:
        (x_ref, gattn_ref, wsm_ref, wg_ref, gq_ref, gkv_ref, wuq_ref, wukbd_ref, cosk_ref, sink_ref,
         cosq_ref, sinq_ref, wconv_ref, gv_ref, bv_ref, ms_ref, bs_ref, f1_ref, f2_ref,
         qlat_ref, qpe_ref, ckv_ref, kpe_ref, kcb_ref, kpb_ref, bconv_ref, xc_ref, cmix_ref, vn_ref,
         sga_ref, sgb_ref, sgc_ref) = refs
        carry_ref = kct_ref = None
    else:
        (x_ref, gattn_ref, wsm_ref, wg_ref, gq_ref, gkv_ref, wuq_ref, wukbd_ref, cosk_ref, sink_ref,
         cosq_ref, sinq_ref, wconv_ref, gv_ref, bv_ref, ms_ref, bs_ref,
         qlat_ref, qpe_ref, ckv_ref, kpe_ref, kcb_ref, kpb_ref, bconv_ref, xc_ref, cmix_ref, vn_ref,
         sga_ref, sgb_ref, sgc_ref, kct_ref, carry_ref) = refs

    x = x_ref[...]
    h = _rms_rows(x, gattn_ref[...]).astype(BF16)

    def proj(a, b):
        return _dot(h, wsm_ref[:, a:b])

    cq = proj(_OFF_CQ, _OFF_CKV)
    cqn = _rms_rows(cq, gq_ref[...]).astype(BF16)
    qall = _dot(cqn, wuq_ref[...])
    n_nope = N_HEADS_A * NOPE_DIM
    n_pe = N_HEADS_A * ROPE_DIM
    qpe = (qall[:, n_nope:n_nope + n_pe] * cosq_ref[...]
           + qall[:, n_nope + n_pe:n_nope + 2 * n_pe] * sinq_ref[...]) * MLA_SCALE
    qlat = _dot(qall[:, :n_nope].astype(BF16), wukbd_ref[...]) * MLA_SCALE
    if sample:
        for hd in range(N_HEADS_A):
            qlat_ref[hd] = qlat[:, hd * KV_RANK:(hd + 1) * KV_RANK]
            qpe_ref[hd] = qpe[:, hd * ROPE_DIM:(hd + 1) * ROPE_DIM]
    else:
        for blk in range(tm // Q_BLOCK):
            r0 = blk * Q_BLOCK
            qpe_t = qpe[r0:r0 + Q_BLOCK, :].T
            for hd in range(N_HEADS_A):
                cols = slice(hd * Q_BLOCK, (hd + 1) * Q_BLOCK)
                qlat_ref[blk, :, cols] = qlat[r0:r0 + Q_BLOCK, hd * KV_RANK:(hd + 1) * KV_RANK].T.astype(BF16)
                qpe_ref[blk, :, cols] = qpe_t[hd * ROPE_DIM:(hd + 1) * ROPE_DIM, :].astype(BF16)

    ckv = _rms_rows(proj(_OFF_CKV, _OFF_CB), gkv_ref[...])
    ckv_ref[...] = ckv
    kcb_ref[...] = ckv.astype(BF16)
    if not sample:
        kct_ref[...] = ckv.T.astype(BF16)
    kpe = proj(_OFF_KPE, _OFF_KPE_SW) * cosk_ref[...] + proj(_OFF_KPE_SW, _N_SMALL) * sink_ref[...]
    kpe_ref[...] = kpe
    kpb_ref[...] = kpe.astype(BF16)

    cb = proj(_OFF_CB, _OFF_CC)
    xc = proj(_OFF_CC, _OFF_CH) * proj(_OFF_CH, _OFF_U)
    xc_ref[...] = xc
    rows = lax.broadcasted_iota(jnp.int32, (tm, 1), 0)
    r1 = pltpu.roll(xc, 1, 0)
    r2 = pltpu.roll(xc, 2, 0)
    if sample:
        pos = rows % 8
        m1 = jnp.where(pos == 0, f1_ref[...], r1)
        m2 = jnp.where(pos < 2, f2_ref[...], r2)
    else:
        first = (pl.program_id(0) % blocks_per_seq) == 0
        prev = jnp.where(first, 0.0, carry_ref[...])
        p6 = prev[6:7, :]
        p7 = prev[7:8, :]
        m1 = jnp.where(rows == 0, p7, r1)
        m2 = jnp.where(rows == 0, p6, jnp.where(rows == 1, p7, r2))
        carry_ref[...] = xc[tm - 8:tm, :]
    y = m2 * wconv_ref[0:1, :] + m1 * wconv_ref[1:2, :] + xc * wconv_ref[2:3, :]
    bconv_ref[...] = (cb * y).astype(BF16)

    u = proj(_OFF_U, _OFF_V)
    v = proj(_OFF_V, _OFF_KPE)
    vc = v - jnp.mean(v, axis=-1, keepdims=True)
    vn = vc * lax.rsqrt(jnp.mean(vc * vc, axis=-1, keepdims=True) + EPS) * gv_ref[...] + bv_ref[...]
    vn_ref[...] = vn
    vnb = vn.astype(BF16)
    gw = W_C // N_GROUPS_C
    for r0 in range(0, tm, CHUNK):
        for g in range(N_GROUPS_C):
            mix = _dot(ms_ref[g], vnb[r0:r0 + CHUNK, g * gw:(g + 1) * gw]) + bs_ref[:, g * gw:(g + 1) * gw]
            cmix_ref[r0:r0 + CHUNK, g * gw:(g + 1) * gw] = (
                u[r0:r0 + CHUNK, g * gw:(g + 1) * gw] * mix).astype(BF16)

    sga_ref[...] = jax.nn.sigmoid(_dot(h, wg_ref[:, 0:d_model]))
    sgb_ref[...] = jax.nn.sigmoid(_dot(h, wg_ref[:, d_model:2 * d_model]))
    sgc_ref[...] = jax.nn.sigmoid(_dot(h, wg_ref[:, 2 * d_model:3 * d_model]))


def _inproj(x, lw, tabs, sample, seq_len, conv_fix, tm):
    t, d = x.shape
    nblk = t // tm
    blocks_per_seq = max(seq_len // tm, 1)
    row = lambda w: pl.BlockSpec((tm, w), lambda i: (i, 0))
    if sample:
        tab_spec = lambda w: pl.BlockSpec((tm, w), lambda i: (0, 0))
    else:
        tab_spec = lambda w: pl.BlockSpec((tm, w), lambda i: (i % blocks_per_seq, 0))
    n_pe = N_HEADS_A * ROPE_DIM
    in_specs = [
        row(d), _const_spec((1, d)), _const_spec(lw["w_small"].shape), _const_spec(lw["w_gates"].shape),
        _const_spec((1, Q_RANK)), _const_spec((1, KV_RANK)), _const_spec(lw["w_uq"].shape),
        _const_spec(lw["w_uk_bd"].shape),
        tab_spec(ROPE_DIM), tab_spec(ROPE_DIM), tab_spec(n_pe), tab_spec(n_pe),
        _const_spec((CONV_W, W_B)), _const_spec((1, W_C)), _const_spec((1, W_C)),
        _const_spec((N_GROUPS_C, CHUNK, CHUNK)), _const_spec((CHUNK, W_C)),
    ]
    args = [x, lw["g_attn"], lw["w_small"], lw["w_gates"], lw["g_q"], lw["g_kv"], lw["w_uq"], lw["w_uk_bd"],
            tabs["cosk"], tabs["sink"], tabs["cosq"], tabs["sinq"], lw["w_conv"], lw["g_v"], lw["b_v"],
            lw["ms_s"] if sample else lw["ms_p"], lw["bs_s"] if sample else lw["bs_p"]]
    scratch = []
    if sample:
        in_specs += [row(W_B), row(W_B)]
        args += [conv_fix[0], conv_fix[1]]
    else:
        scratch = [pltpu.VMEM((8, W_B), F32)]
    n_qcols = N_HEADS_A * Q_BLOCK
    if sample:
        q_shapes = [jax.ShapeDtypeStruct((N_HEADS_A, t, KV_RANK), F32),
                    jax.ShapeDtypeStruct((N_HEADS_A, t, ROPE_DIM), F32)]
        q_specs = [pl.BlockSpec((N_HEADS_A, tm, KV_RANK), lambda i: (0, i, 0)),
                   pl.BlockSpec((N_HEADS_A, tm, ROPE_DIM), lambda i: (0, i, 0))]
    else:
        q_shapes = [jax.ShapeDtypeStruct((t // Q_BLOCK, KV_RANK, n_qcols), BF16),
                    jax.ShapeDtypeStruct((t // Q_BLOCK, ROPE_DIM, n_qcols), BF16)]
        q_specs = [pl.BlockSpec((tm // Q_BLOCK, KV_RANK, n_qcols), lambda i: (i, 0, 0)),
                   pl.BlockSpec((tm // Q_BLOCK, ROPE_DIM, n_qcols), lambda i: (i, 0, 0))]
    out_shape = q_shapes + [
        jax.ShapeDtypeStruct((t, KV_RANK), F32),
        jax.ShapeDtypeStruct((t, ROPE_DIM), F32),
        jax.ShapeDtypeStruct((t, KV_RANK), BF16),
        jax.ShapeDtypeStruct((t, ROPE_DIM), BF16),
        jax.ShapeDtypeStruct((t, W_B), BF16),
        jax.ShapeDtypeStruct((t, W_B), F32),
        jax.ShapeDtypeStruct((t, W_C), BF16),
        jax.ShapeDtypeStruct((t, W_C), F32),
        jax.ShapeDtypeStruct((t, d), F32),
        jax.ShapeDtypeStruct((t, d), F32),
        jax.ShapeDtypeStruct((t, d), F32),
    ]
    out_specs = q_specs + [row(KV_RANK), row(ROPE_DIM), row(KV_RANK),
                           row(ROPE_DIM), row(W_B), row(W_B), row(W_C), row(W_C), row(d), row(d), row(d)]
    if not sample:
        out_shape.append(jax.ShapeDtypeStruct((KV_RANK, t), BF16))
        out_specs.append(pl.BlockSpec((KV_RANK, tm), lambda i: (0, i)))
    return pl.pallas_call(
        functools.partial(_inproj_body, sample, tm, blocks_per_seq, d),
        grid=(nblk,),
        in_specs=in_specs,
        out_specs=out_specs,
        out_shape=out_shape,
        scratch_shapes=scratch,
        compiler_params=_cparams(("arbitrary",)),
        name="inproj_sample" if sample else "inproj_prompt",
    )(*args)


def _softmax_step(s, k_bf, m_ref, l_ref, acc_ref):
    m_prev = m_ref[...]
    m_next = jnp.maximum(m_prev, jnp.max(s, axis=1, keepdims=True))
    p = jnp.exp(s - m_next)
    alpha = jnp.exp(m_prev - m_next)
    l_ref[...] = l_ref[...] * alpha + jnp.sum(p, axis=1, keepdims=True)
    acc_ref[...] = acc_ref[...] * alpha + _dot(p.astype(BF16), k_bf)
    m_ref[...] = m_next


ATTN_COL_GROUP = 256


def _attn_prompt_body(kb, qlt_ref, qpt_ref, kc_ref, kp_ref, kct_ref, o_ref, m_ref, l_ref, acc_ref):
    qi = pl.program_id(1)
    kj = pl.program_id(2)
    cols_all = N_HEADS_A * Q_BLOCK
    last = ((qi + 1) * Q_BLOCK - 1) // kb

    @pl.when(kj == 0)
    def _():
        m_ref[...] = jnp.full((1, cols_all), NEG_BIG, F32)
        l_ref[...] = jnp.zeros((1, cols_all), F32)
        acc_ref[...] = jnp.zeros((KV_RANK, cols_all), F32)

    def step(masked):
        kc = kc_ref[...]
        kp = kp_ref[...]
        kct = kct_ref[...]
        for c0 in range(0, cols_all, ATTN_COL_GROUP):
            cols = slice(c0, c0 + ATTN_COL_GROUP)
            s = _dot(kc, qlt_ref[:, cols]) + _dot(kp, qpt_ref[:, cols])
            if masked:
                kpos = kj * kb + lax.broadcasted_iota(jnp.int32, (kb, 1), 0)
                qpos = qi * Q_BLOCK + lax.broadcasted_iota(jnp.int32, (1, ATTN_COL_GROUP), 1) % Q_BLOCK
                s = jnp.where(kpos <= qpos, s, NEG_BIG)
            m_prev = m_ref[:, cols]
            m_next = jnp.maximum(m_prev, jnp.max(s, axis=0, keepdims=True))
            p = jnp.exp(s - m_next)
            alpha = jnp.exp(m_prev - m_next)
            l_ref[:, cols] = l_ref[:, cols] * alpha + jnp.sum(p, axis=0, keepdims=True)
            acc_ref[:, cols] = acc_ref[:, cols] * alpha + _dot(kct, p.astype(BF16))
            m_ref[:, cols] = m_next

    @pl.when(kj < last)
    def _():
        step(False)

    @pl.when(kj == last)
    def _():
        step(True)
        o_t = acc_ref[...] / l_ref[...]
        for hd in range(N_HEADS_A):
            o_ref[hd] = o_t[:, hd * Q_BLOCK:(hd + 1) * Q_BLOCK].T.astype(BF16)


def _attn_prompt(qlt, qpt, kcb, kpb, kct, batch, seq, kb):
    nq = seq // Q_BLOCK
    nk = seq // kb
    t = batch * seq
    cols_all = N_HEADS_A * Q_BLOCK

    def kblk(b, qi, kj):
        return b * nk + jnp.minimum(kj, ((qi + 1) * Q_BLOCK - 1) // kb)

    qmap = lambda b, qi, kj: (b * nq + qi, 0, 0)
    return pl.pallas_call(
        functools.partial(_attn_prompt_body, kb),
        grid=(batch, nq, nk),
        in_specs=[
            pl.BlockSpec((None, KV_RANK, cols_all), qmap),
            pl.BlockSpec((None, ROPE_DIM, cols_all), qmap),
            pl.BlockSpec((kb, KV_RANK), lambda b, qi, kj: (kblk(b, qi, kj), 0)),
            pl.BlockSpec((kb, ROPE_DIM), lambda b, qi, kj: (kblk(b, qi, kj), 0)),
            pl.BlockSpec((KV_RANK, kb), lambda b, qi, kj: (0, kblk(b, qi, kj))),
        ],
        out_specs=pl.BlockSpec((N_HEADS_A, Q_BLOCK, KV_RANK), lambda b, qi, kj: (0, b * nq + qi, 0)),
        out_shape=jax.ShapeDtypeStruct((N_HEADS_A, t, KV_RANK), BF16),
        scratch_shapes=[pltpu.VMEM((1, cols_all), F32), pltpu.VMEM((1, cols_all), F32),
                        pltpu.VMEM((KV_RANK, cols_all), F32)],
        compiler_params=_cparams(("arbitrary", "arbitrary", "arbitrary")),
        name="attn_prompt",
    )(qlt, qpt, kcb, kpb, kct)


def _attn_sample_body(pp, dec_seq, page, *refs):
    pt_ref = refs[0]
    del pt_ref
    qlat_ref, qpe_ref = refs[1], refs[2]
    c_refs = refs[3:3 + pp]
    p_refs = refs[3 + pp:3 + 2 * pp]
    knc_ref, knp_ref = refs[3 + 2 * pp], refs[4 + 2 * pp]
    o_ref = refs[5 + 2 * pp]
    kbuf_ref, pbuf_ref, m_ref, l_ref, acc_ref = refs[6 + 2 * pp:]
    j = pl.program_id(1)
    nj = pl.num_programs(1)
    rows = N_HEADS_A * dec_seq

    @pl.when(j == 0)
    def _():
        m_ref[...] = jnp.full((rows, 1), NEG_BIG, F32)
        l_ref[...] = jnp.zeros((rows, 1), F32)
        acc_ref[...] = jnp.zeros((rows, KV_RANK), F32)

    q = qlat_ref[...].reshape(rows, KV_RANK).astype(BF16)
    qp = qpe_ref[...].reshape(rows, ROPE_DIM).astype(BF16)
    for p in range(pp):
        kbuf_ref[p * page:(p + 1) * page, :] = c_refs[p][...].astype(BF16)
        pbuf_ref[p * page:(p + 1) * page, :] = p_refs[p][...].astype(BF16)
    k = kbuf_ref[...]
    s = _dot_nt(q, k) + _dot_nt(qp, pbuf_ref[...])
    _softmax_step(s, k, m_ref, l_ref, acc_ref)

    @pl.when(j == nj - 1)
    def _():
        kn = knc_ref[...]
        sn = _dot_nt(q, kn) + _dot_nt(qp, knp_ref[...])
        qpos = lax.broadcasted_iota(jnp.int32, (rows, 1), 0) % dec_seq
        kpos = lax.broadcasted_iota(jnp.int32, (1, page), 1)
        sn = jnp.where(kpos <= qpos, sn, NEG_BIG)
        _softmax_step(sn, kn, m_ref, l_ref, acc_ref)
        o = acc_ref[...] / l_ref[...]
        o_ref[...] = o.reshape(N_HEADS_A, dec_seq, KV_RANK)


def _attn_sample(layer, qlat, qpe, cache_ckv, cache_kpe, page_table, knew_c, knew_p, dec_seq, pp):
    dec_b, n_pages = page_table.shape
    page = cache_ckv.shape[2]
    t = dec_b * dec_seq
    qmap = lambda b, j, pt: (0, b, 0)

    def cmap(p):
        return lambda b, j, pt: (layer, pt[b, j * pp + p], 0, 0)

    in_specs = [pl.BlockSpec((N_HEADS_A, dec_seq, KV_RANK), qmap),
                pl.BlockSpec((N_HEADS_A, dec_seq, ROPE_DIM), qmap)]
    in_specs += [pl.BlockSpec((None, None, page, KV_RANK), cmap(p)) for p in range(pp)]
    in_specs += [pl.BlockSpec((None, None, page, ROPE_DIM), cmap(p)) for p in range(pp)]
    in_specs += [pl.BlockSpec((None, page, KV_RANK), lambda b, j, pt: (b, 0, 0)),
                 pl.BlockSpec((None, page, ROPE_DIM), lambda b, j, pt: (b, 0, 0))]
    rows = N_HEADS_A * dec_seq
    grid_spec = pltpu.PrefetchScalarGridSpec(
        num_scalar_prefetch=1,
        grid=(dec_b, n_pages // pp),
        in_specs=in_specs,
        out_specs=pl.BlockSpec((N_HEADS_A, dec_seq, KV_RANK), qmap),
        scratch_shapes=[pltpu.VMEM((pp * page, KV_RANK), BF16), pltpu.VMEM((pp * page, ROPE_DIM), BF16),
                        pltpu.VMEM((rows, 1), F32), pltpu.VMEM((rows, 1), F32),
                        pltpu.VMEM((rows, KV_RANK), F32)],
    )
    return pl.pallas_call(
        functools.partial(_attn_sample_body, pp, dec_seq, page),
        grid_spec=grid_spec,
        out_shape=jax.ShapeDtypeStruct((N_HEADS_A, t, KV_RANK), F32),
        compiler_params=_cparams(("arbitrary", "arbitrary")),
        name="attn_sample",
    )(page_table, qlat, qpe, *([cache_ckv] * pp), *([cache_kpe] * pp), knew_c, knew_p)


def _merge_body(x_ref, olat_ref, wuvbd_ref, bconv_ref, cmix_ref, sga_ref, sgb_ref, sgc_ref,
                wbra_ref, wbrb_ref, wbrc_ref, wout_ref, gffn_ref, wpq_ref, skeys_ref,
                x1_ref, h2t_ref, st_ref):
    ocat = jnp.concatenate([olat_ref[hd].astype(BF16) for hd in range(N_HEADS_A)], axis=1)
    a = _dot(ocat, wuvbd_ref[...]).astype(BF16)
    merged = (sga_ref[...] * _dot(a, wbra_ref[...])
              + sgb_ref[...] * _dot(bconv_ref[...], wbrb_ref[...])
              + sgc_ref[...] * _dot(cmix_ref[...], wbrc_ref[...]))
    x1 = x_ref[...] + _dot(merged.astype(BF16), wout_ref[...])
    x1_ref[...] = x1
    h2 = _rms_rows(x1, gffn_ref[...])
    h2t_ref[...] = h2.T.astype(BF16)
    q = _dot(h2.astype(BF16), wpq_ref[...]).astype(BF16)
    half = q.shape[1] // (2 * N_HEADS_P)
    for hc in range(2 * N_HEADS_P):
        st_ref[hc * N_KEYS:(hc + 1) * N_KEYS, :] = _dot_nt(skeys_ref[hc], q[:, hc * half:(hc + 1) * half])


def _merge(x, olat, bconv, cmix, sga, sgb, sgc, lw, tm):
    t, d = x.shape
    row = lambda w: pl.BlockSpec((tm, w), lambda i: (i, 0))
    col = lambda r: pl.BlockSpec((r, tm), lambda i: (0, i))
    n_s = 2 * N_HEADS_P * N_KEYS
    return pl.pallas_call(
        _merge_body,
        grid=(t // tm,),
        in_specs=[row(d), pl.BlockSpec((N_HEADS_A, tm, KV_RANK), lambda i: (0, i, 0)),
                  _const_spec(lw["w_uv_bd"].shape), row(W_B), row(W_C), row(d), row(d), row(d),
                  _const_spec(lw["w_br_a"].shape), _const_spec(lw["w_br_b"].shape),
                  _const_spec(lw["w_br_c"].shape), _const_spec(lw["w_out"].shape), _const_spec((1, d)),
                  _const_spec(lw["w_pq"].shape), _const_spec(lw["skeys"].shape)],
        out_specs=[row(d), col(d), col(n_s)],
        out_shape=[jax.ShapeDtypeStruct((t, d), F32), jax.ShapeDtypeStruct((d, t), BF16),
                   jax.ShapeDtypeStruct((n_s, t), F32)],
        compiler_params=_cparams(("arbitrary",)),
        name="merge",
    )(x, olat, lw["w_uv_bd"], bconv, cmix, sga, sgb, sgc, lw["w_br_a"], lw["w_br_b"], lw["w_br_c"],
      lw["w_out"], lw["g_ffn"], lw["w_pq"], lw["skeys"])


def _top_values(cur, n, out_ref, base):
    for k in range(n):
        mx = jnp.max(cur, axis=0, keepdims=True)
        out_ref[base + k:base + k + 1, :] = mx
        if k + 1 < n:
            cur = jnp.where(cur == mx, -jnp.inf, cur)


def _peer_select_body(st_ref, a0_ref, b1_ref, cthr_ref, top_ref, cand_ref):
    tn = st_ref.shape[1]
    n_half = N_HEADS_P * N_KEYS
    for hc in range(2 * N_HEADS_P):
        _top_values(st_ref[hc * N_KEYS:(hc + 1) * N_KEYS, :], TOPK_P, top_ref, hc * TOPK_P)
    rank = lax.broadcasted_iota(jnp.int32, (TOPK_P, 1), 0)
    for hd in range(N_HEADS_P):
        s0 = top_ref[hd * TOPK_P:(hd + 1) * TOPK_P, :]
        s1 = top_ref[(N_HEADS_P + hd) * TOPK_P:(N_HEADS_P + hd + 1) * TOPK_P, :]
        half = TOPK_P // 2
        s0_lo = s0[0:half, :]
        part0 = s0 + s1[0:1, :]
        mid = [jnp.where(rank[0:half] < TOPK_P // (b + 1), s0_lo + s1[b:b + 1, :], -jnp.inf)
               for b in range(1, half)]
        tail = s0[0:1, :] + s1[half:TOPK_P, :]
        cands = jnp.concatenate([part0] + mid + [tail], axis=0)
        _top_values(cands, TOPK_P, cand_ref, 0)
        thr = cand_ref[TOPK_P - 1:TOPK_P, :]
        mx = cand_ref[0:1, :]
        z = jnp.sum(jnp.where(cands >= thr, jnp.exp(cands - mx), 0.0), axis=0, keepdims=True)
        tau_lo = jnp.full((half, tn), jnp.inf, F32)
        for b in range(1, half):
            tau_lo = jnp.minimum(tau_lo, jnp.where(mid[b - 1] >= thr, s1[b:b + 1, :], jnp.inf))
        tau_0 = jnp.min(jnp.where(tail >= thr, s1[half:TOPK_P, :], jnp.inf), axis=0, keepdims=True)
        tau_lo = jnp.minimum(tau_lo, jnp.where(rank[0:half] == 0, tau_0, jnp.inf))
        tau = jnp.minimum(jnp.where(part0 >= thr, s1[0:1, :], jnp.inf),
                          jnp.concatenate([tau_lo, jnp.full((half, tn), jnp.inf, F32)], axis=0))
        s0_all = st_ref[hd * N_KEYS:(hd + 1) * N_KEYS, :]
        s1_all = st_ref[n_half + hd * N_KEYS:n_half + (hd + 1) * N_KEYS, :]

        def factor1(x):
            return jnp.exp(x - s1[0:1, :]) * (0.5 / z)

        beta = factor1(tau)
        cthr = jnp.full((N_KEYS, tn), jnp.inf, F32)
        for a in reversed(range(TOPK_P)):
            cthr = jnp.where(s0_all >= s0[a:a + 1, :], beta[a:a + 1, :], cthr)
        cthr_ref[hd * N_KEYS:(hd + 1) * N_KEYS, :] = cthr
        a0_ref[hd * N_KEYS:(hd + 1) * N_KEYS, :] = jnp.exp(s0_all - s0[0:1, :])
        b1_ref[hd * N_KEYS:(hd + 1) * N_KEYS, :] = factor1(s1_all)


def _peer_select(st, tn):
    n_s, t = st.shape
    n_h = N_HEADS_P * N_KEYS
    col = lambda r: pl.BlockSpec((r, tn), lambda i: (0, i))
    return pl.pallas_call(
        _peer_select_body,
        grid=(t // tn,),
        in_specs=[col(n_s)],
        out_specs=[col(n_h), col(n_h), col(n_h)],
        out_shape=[jax.ShapeDtypeStruct((n_h, t), F32)] * 3,
        scratch_shapes=[pltpu.VMEM((2 * N_HEADS_P * TOPK_P, tn), F32), pltpu.VMEM((TOPK_P, tn), F32)],
        compiler_params=_cparams(("arbitrary",)),
        name="peer_select",
    )(st)


def _peer_gate_block(item, at, a0_ref, b1_ref, cthr_ref, w_ref, keys_per_block, lane_chunk):
    tn = at.shape[1]
    for c0 in range(0, tn, lane_chunk):
        lanes = slice(c0, c0 + lane_chunk)
        for ii in range(keys_per_block):
            gate = None
            for hd in range(N_HEADS_P):
                b1 = b1_ref[hd * N_KEYS:(hd + 1) * N_KEYS, lanes]
                cthr = cthr_ref[hd, item, ii:ii + 1, lanes]
                a0 = a0_ref[hd, item, ii:ii + 1, lanes]
                term = a0 * jnp.where(b1 >= cthr, b1, 0.0)
                gate = term if gate is None else gate + term
            xa = at[ii * N_KEYS:(ii + 1) * N_KEYS, lanes]
            act2 = xa * (1.0 + lax.erf(xa * (2.0 ** -0.5)))
            w_ref[ii * N_KEYS:(ii + 1) * N_KEYS, lanes] = (gate * act2).astype(BF16)


def _peer_dense_body(eb, lane_chunk, h2t_ref, u_ref, vta_ref, vtb_ref, vtc_ref, a0_ref, b1_ref,
                     cthr_ref, out_ref, w0_ref, w1_ref):
    k = pl.program_id(1)
    nk = pl.num_programs(1)
    keys_per_block = eb // N_KEYS
    gate_args = (a0_ref, b1_ref, cthr_ref)

    @pl.when(k == 0)
    def _():
        out_ref[...] = jnp.zeros(out_ref.shape, F32)
        w1_ref[...] = jnp.zeros(w1_ref.shape, BF16)

    h2t = h2t_ref[...]
    at0 = _dot(u_ref[0:eb, :], h2t)
    out_ref[...] += _dot(vta_ref[...], w1_ref[...])
    _peer_gate_block(0, at0, *gate_args, w0_ref, keys_per_block, lane_chunk)
    at1 = _dot(u_ref[eb:2 * eb, :], h2t)
    out_ref[...] += _dot(vtb_ref[...], w0_ref[...])
    _peer_gate_block(1, at1, *gate_args, w1_ref, keys_per_block, lane_chunk)

    @pl.when(k == nk - 1)
    def _():
        out_ref[...] += _dot(vtc_ref[...], w1_ref[...])


def _peer_dense(h2t, u_bf, vt_bf, a0, b1, cthr, tn, eb):
    d, t = h2t.shape
    n_e = u_bf.shape[0]
    n_blocks = n_e // eb
    assert eb == 8 * N_KEYS
    assert n_blocks % 2 == 0
    col = lambda r: pl.BlockSpec((r, tn), lambda i, k: (0, i))
    n_h = N_HEADS_P * N_KEYS
    keys_per_block = eb // N_KEYS
    by_block = lambda a: a.reshape(N_HEADS_P, n_blocks, keys_per_block, t)
    blk_rows = pl.BlockSpec((N_HEADS_P, 2, keys_per_block, tn), lambda i, k: (0, k, 0, i))
    return pl.pallas_call(
        functools.partial(_peer_dense_body, eb, 128),
        grid=(t // tn, n_blocks // 2),
        in_specs=[col(d), pl.BlockSpec((2 * eb, d), lambda i, k: (k, 0)),
                  pl.BlockSpec((d, eb), lambda i, k: (0, jnp.maximum(2 * k - 1, 0))),
                  pl.BlockSpec((d, eb), lambda i, k: (0, 2 * k)),
                  pl.BlockSpec((d, eb), lambda i, k: (0, n_blocks - 1)),
                  blk_rows, col(n_h), blk_rows],
        out_specs=col(d),
        out_shape=jax.ShapeDtypeStruct((d, t), F32),
        scratch_shapes=[pltpu.VMEM((eb, tn), BF16), pltpu.VMEM((eb, tn), BF16)],
        compiler_params=_cparams(("arbitrary", "arbitrary")),
        name="peer_dense",
    )(h2t, u_bf, vt_bf, vt_bf, vt_bf, by_block(a0), b1, by_block(cthr))


def _table_prep_body(u_ref, v_ref, ub_ref, vt_ref):
    ub_ref[...] = u_ref[...].astype(BF16)
    vt_ref[...] = v_ref[...].T.astype(BF16)


def _table_prep(layer, u_tab, v_tab, rows):
    _, n_e, d = u_tab.shape
    src = pl.BlockSpec((None, rows, d), lambda e: (layer, e, 0))
    return pl.pallas_call(
        _table_prep_body,
        grid=(n_e // rows,),
        in_specs=[src, src],
        out_specs=[pl.BlockSpec((rows, d), lambda e: (e, 0)), pl.BlockSpec((d, rows), lambda e: (0, e))],
        out_shape=[jax.ShapeDtypeStruct((n_e, d), BF16), jax.ShapeDtypeStruct((d, n_e), BF16)],
        compiler_params=_cparams(("arbitrary",)),
        name="table_prep",
    )(u_tab, v_tab)


def _ple_body(last, x1_ref, pt_ref, gple_ref, wpg_ref, p_ref, wple_ref, *rest):
    if last:
        gfin_ref, x3_ref, y_ref = rest
    else:
        (x3_ref,) = rest
    x2 = x1_ref[...] + pt_ref[...].T
    hp = _rms_rows(x2, gple_ref[...]).astype(BF16)
    gate = jax.nn.sigmoid(_dot(hp, wpg_ref[...]))
    x3 = x2 + gate * _dot(p_ref[...].astype(BF16), wple_ref[...])
    x3_ref[...] = x3
    if last:
        y_ref[...] = _rms_rows(x3, gfin_ref[...])


def _ple(x1, peer_t, col_off, p_i, lw, g_final, last, tm):
    t, d = x1.shape
    row = lambda w: pl.BlockSpec((tm, w), lambda i: (i, 0))
    in_specs = [row(d), pl.BlockSpec((d, tm), lambda i: (0, i + col_off)), _const_spec((1, d)),
                _const_spec(lw["w_pg"].shape), row(p_i.shape[1]), _const_spec(lw["w_ple"].shape)]
    args = [x1, peer_t, lw["g_ple"], lw["w_pg"], p_i, lw["w_ple"]]
    out_specs = [row(d)]
    out_shape = [jax.ShapeDtypeStruct((t, d), F32)]
    if last:
        in_specs.append(_const_spec((1, d)))
        args.append(g_final)
        out_specs.append(row(d))
        out_shape.append(jax.ShapeDtypeStruct((t, d), F32))
    return pl.pallas_call(
        functools.partial(_ple_body, last),
        grid=(t // tm,),
        in_specs=in_specs,
        out_specs=out_specs,
        out_shape=out_shape,
        compiler_params=_cparams(("arbitrary",)),
        name="ple_last" if last else "ple",
    )(*args)


def _rope_tables(pos):
    inv = 1.0 / (ROPE_THETA ** (jnp.arange(0, ROPE_DIM, 2, dtype=F32) / ROPE_DIM))
    ang = pos.astype(F32)[:, None] * inv[None, :]
    cos, sin = jnp.cos(ang), jnp.sin(ang)
    cosk = jnp.concatenate([cos, cos], axis=1)
    sink = jnp.concatenate([-sin, sin], axis=1)
    return {"cosk": cosk, "sink": sink, "cosq": jnp.tile(cosk, (1, N_HEADS_A)),
            "sinq": jnp.tile(sink, (1, N_HEADS_A))}


def _swap_halves(w):
    half = w.shape[-1] // 2
    return jnp.concatenate([w[..., half:], w[..., :half]], axis=-1)


def _block_diag(blocks):
    n, r, c = blocks.shape
    eye = jnp.eye(n, dtype=blocks.dtype)
    return (eye[:, None, :, None] * blocks[:, :, None, :]).reshape(n * r, n * c)


def _layer_weights(i, d, dec_seq, g_attn, w_in, g_q, g_kv, w_uq, w_uk, w_uv, w_conv, g_v, b_v, w_s, b_s,
                   w_br_a, w_br_b, w_br_c, w_out, g_ffn, w_pq, sub_keys, u_tab, v_tab, g_ple, w_pg, w_ple):
    bf = lambda a: a.astype(BF16)
    wi = w_in[i]
    o_kpe = Q_RANK + KV_RANK
    o_cb = o_kpe + ROPE_DIM
    o_g = o_cb + 3 * W_B + 2 * W_C
    w_kpe = wi[:, o_kpe:o_cb]
    w_small = jnp.concatenate([wi[:, :o_kpe], wi[:, o_cb:o_g], w_kpe, _swap_halves(w_kpe)], axis=1)
    uq = w_uq[i]
    uq_pe = uq[:, :, NOPE_DIM:]
    w_uq_p = jnp.concatenate([uq[:, :, :NOPE_DIM].reshape(Q_RANK, -1), uq_pe.reshape(Q_RANK, -1),
                              _swap_halves(uq_pe).reshape(Q_RANK, -1)], axis=1)
    ws = jnp.tril(w_s[i])
    ws_small = jnp.tril(w_s[i][:, :dec_seq, :dec_seq])
    eye = jnp.eye(CHUNK // dec_seq, dtype=F32)
    ms_s = jnp.einsum("ab,gts->gatbs", eye, ws_small).reshape(N_GROUPS_C, CHUNK, CHUNK)
    gw = W_C // N_GROUPS_C
    bs_p = jnp.repeat(b_s[i].T, gw, axis=1)
    bs_s = jnp.tile(bs_p[:dec_seq], (CHUNK // dec_seq, 1))
    return {
        "g_attn": g_attn[i][None], "w_small": bf(w_small), "w_gates": bf(wi[:, o_g:]),
        "g_q": g_q[i][None], "g_kv": g_kv[i][None], "w_uq": bf(w_uq_p),
        "w_uk_bd": bf(_block_diag(jnp.transpose(w_uk[i], (1, 2, 0)))),
        "w_uv_bd": bf(_block_diag(jnp.transpose(w_uv[i], (1, 0, 2)))),
        "w_conv": w_conv[i], "g_v": g_v[i][None], "b_v": b_v[i][None],
        "ms_p": bf(ws), "ms_s": bf(ms_s), "bs_p": bs_p, "bs_s": bs_s,
        "w_br_a": bf(w_br_a[i]), "w_br_b": bf(w_br_b[i]), "w_br_c": bf(w_br_c[i]), "w_out": bf(w_out[i]),
        "g_ffn": g_ffn[i][None], "w_pq": bf(jnp.transpose(w_pq[i], (0, 2, 1, 3)).reshape(d, -1)),
        "skeys": bf(jnp.transpose(sub_keys[i], (1, 0, 2, 3)).reshape(2 * N_HEADS_P, N_KEYS, -1)),
        "g_ple": g_ple[i][None], "w_pg": bf(w_pg[i]), "w_ple": bf(w_ple[i]),
    }


def _pick_tile(n, prefs):
    for p in prefs:
        if n % p == 0:
            return p
    raise ValueError(f"no tile for {n}")


def kernel(x_prompt, x_sample, p_prompt, p_sample, cache_ckv, cache_kpe, state_conv, page_table,
           g_attn, w_in, g_q, g_kv, w_uq, w_uk, w_uv, w_conv, g_v, b_v, w_s, b_s,
           w_br_a, w_br_b, w_br_c, w_out, g_ffn, w_pq, sub_keys, u_tab, v_tab,
           g_ple, w_pg, w_ple, g_final):
    batch, seq, d = x_prompt.shape
    dec_b, dec_seq, _ = x_sample.shape
    depth = w_in.shape[0]
    n_pages = page_table.shape[1]
    page = cache_ckv.shape[2]
    past_len = n_pages * page
    tp, ts = batch * seq, dec_b * dec_seq
    assert seq % CHUNK == 0 and CHUNK % dec_seq == 0 and ts % CHUNK == 0 and dec_seq == 8
    assert page == CHUNK

    tm_p = _pick_tile(seq, (256, 128))
    tm_s = _pick_tile(ts, (256, 128))
    kb = _pick_tile(seq, (512, 256, 128))
    pp = _pick_tile(n_pages, (16, 8, 4, 2, 1))
    tn_p = _pick_tile(tp, (512, 256, 128))
    tn_s = _pick_tile(ts, (512, 256, 128))
    eb = 1024

    tabs_p = _rope_tables(jnp.arange(seq, dtype=jnp.int32))
    tabs_s = _rope_tables(past_len + (jnp.arange(tm_s, dtype=jnp.int32) % dec_seq))
    g_fin = g_final[None]

    xp = x_prompt.reshape(tp, d)
    xs = x_sample.reshape(ts, d)
    outs = {k: [] for k in ("ckv_p", "kpe_p", "ckv_s", "kpe_s", "conv_p", "conv_s", "v_s")}
    y_p = y_s = None
    for i in range(depth):
        lw = _layer_weights(i, d, dec_seq, g_attn, w_in, g_q, g_kv, w_uq, w_uk, w_uv, w_conv, g_v, b_v,
                            w_s, b_s, w_br_a, w_br_b, w_br_c, w_out, g_ffn, w_pq, sub_keys, u_tab, v_tab,
                            g_ple, w_pg, w_ple)
        last = i == depth - 1

        (qlt_p, qpt_p, ckv_p, kpe_p, kcb_p, kpb_p, bconv_p, xc_p, cmix_p, _vn_p, sga_p, sgb_p, sgc_p,
         kct_p) = _inproj(xp, lw, tabs_p, False, seq, None, tm_p)
        olat_p = _attn_prompt(qlt_p, qpt_p, kcb_p, kpb_p, kct_p, batch, seq, kb)
        x1_p, h2t_p, st_p = _merge(xp, olat_p, bconv_p, cmix_p, sga_p, sgb_p, sgc_p, lw, tm_p)

        st_c = state_conv[i]
        zero_row = jnp.zeros((dec_b, 1, W_B), F32)
        f1 = jnp.concatenate([st_c[:, 1:2], jnp.zeros((dec_b, dec_seq - 1, W_B), F32)], axis=1).reshape(ts, W_B)
        f2 = jnp.concatenate([st_c[:, 0:1], st_c[:, 1:2], jnp.zeros((dec_b, dec_seq - 2, W_B), F32)],
                             axis=1).reshape(ts, W_B)
        del zero_row
        (qlat_s, qpe_s, ckv_s, kpe_s, kcb_s, kpb_s, bconv_s, xc_s, cmix_s, vn_s, sga_s, sgb_s, sgc_s) = _inproj(
            xs, lw, tabs_s, True, dec_seq, (f1, f2), tm_s)
        knew_c = jnp.pad(kcb_s.reshape(dec_b, dec_seq, KV_RANK), ((0, 0), (0, page - dec_seq), (0, 0)))
        knew_p = jnp.pad(kpb_s.reshape(dec_b, dec_seq, ROPE_DIM), ((0, 0), (0, page - dec_seq), (0, 0)))
        olat_s = _attn_sample(i, qlat_s, qpe_s, cache_ckv, cache_kpe, page_table, knew_c, knew_p, dec_seq, pp)
        x1_s, h2t_s, st_s = _merge(xs, olat_s, bconv_s, cmix_s, sga_s, sgb_s, sgc_s, lw, tm_s)

        u_bf, vt_bf = _table_prep(i, u_tab, v_tab, eb)
        peer_p = _peer_dense(h2t_p, u_bf, vt_bf, *_peer_select(st_p, tn_p), tn_p, eb)
        peer_s = _peer_dense(h2t_s, u_bf, vt_bf, *_peer_select(st_s, tn_s), tn_s, eb)

        res_p = _ple(x1_p, peer_p, 0, p_prompt[i].reshape(tp, -1), lw, g_fin, last, tm_p)
        res_s = _ple(x1_s, peer_s, 0, p_sample[i].reshape(ts, -1), lw, g_fin, last, tm_s)
        xp, xs = res_p[0], res_s[0]
        if last:
            y_p, y_s = res_p[1], res_s[1]

        outs["ckv_p"].append(ckv_p.reshape(batch, seq, KV_RANK))
        outs["kpe_p"].append(kpe_p.reshape(batch, seq, ROPE_DIM))
        outs["ckv_s"].append(ckv_s.reshape(dec_b, dec_seq, KV_RANK))
        outs["kpe_s"].append(kpe_s.reshape(dec_b, dec_seq, ROPE_DIM))
        outs["conv_p"].append(xc_p.reshape(batch, seq, W_B)[:, seq - (CONV_W - 1):])
        outs["conv_s"].append(xc_s.reshape(dec_b, dec_seq, W_B)[:, dec_seq - (CONV_W - 1):])
        outs["v_s"].append(vn_s.reshape(dec_b, dec_seq, W_C))

    return (y_p.reshape(batch, seq, d), y_s.reshape(dec_b, dec_seq, d),
            jnp.stack(outs["ckv_p"]), jnp.stack(outs["kpe_p"]), jnp.stack(outs["ckv_s"]),
            jnp.stack(outs["kpe_s"]), jnp.stack(outs["conv_p"]), jnp.stack(outs["conv_s"]),
            jnp.stack(outs["v_s"]))
```

```python
import functools

import jax
import jax.numpy as jnp
from jax import lax
from jax.experimental import pallas as pl
from jax.experimental.pallas import tpu as pltpu

F32 = jnp.float32
BF16 = jnp.bfloat16

N_HEADS_A = 8
NOPE_DIM = 64
ROPE_DIM = 32
V_DIM = 64
Q_RANK = 768
KV_RANK = 256
ROPE_THETA = 10000.0
Q_BLOCK = 128
MLA_SCALE = (NOPE_DIM + ROPE_DIM) ** -0.5
W_B = 512
CONV_W = 3
W_C = 512
N_GROUPS_C = 4
CHUNK = 128
N_KEYS = 128
N_HEADS_P = 8
TOPK_P = 16
EPS = 1e-6

NEG_BIG = -1e30
VMEM_LIMIT_BYTES = 56 * 1024 * 1024

_OFF_CQ = 0
_OFF_CKV = _OFF_CQ + Q_RANK
_OFF_CB = _OFF_CKV + KV_RANK
_OFF_CC = _OFF_CB + W_B
_OFF_CH = _OFF_CC + W_B
_OFF_U = _OFF_CH + W_B
_OFF_V = _OFF_U + W_C
_OFF_KPE = _OFF_V + W_C
_OFF_KPE_SW = _OFF_KPE + ROPE_DIM
_N_SMALL = _OFF_KPE_SW + ROPE_DIM


def _cparams(semantics):
    return pltpu.CompilerParams(dimension_semantics=semantics, vmem_limit_bytes=VMEM_LIMIT_BYTES)


def _rms_rows(x, g):
    return x * lax.rsqrt(jnp.mean(x * x, axis=-1, keepdims=True) + EPS) * g


def _dot(a, b):
    return jnp.dot(a, b, preferred_element_type=F32)


def _dot_nt(a, b):
    return lax.dot_general(a, b, (((1,), (1,)), ((), ())), preferred_element_type=F32)


def _const_spec(shape):
    nd = len(shape)
    return pl.BlockSpec(shape, lambda *_: (0,) * nd)


def _inproj_body(sample, tm, blocks_per_seq, d_model, *refs):
    if sample:
        (x_ref, gattn_ref, wsm_ref, wg_ref, gq_ref, gkv_ref, wuq_ref, wukbd_ref, cosk_ref, sink_ref,
         cosq_ref, sinq_ref, wconv_ref, gv_ref, bv_ref, ms_ref, bs_ref, f1_ref, f2_ref,
         qlat_ref, qpe_ref, ckv_ref, kpe_ref, kcb_ref, kpb_ref, bconv_ref, xc_ref, cmix_ref, vn_ref,
         sga_ref, sgb_ref, sgc_ref) = refs
        carry_ref = kct_ref = None
    else:
        (x_ref, gattn_ref, wsm_ref, wg_ref, gq_ref, gkv_ref, wuq_ref, wukbd_ref, cosk_ref, sink_ref,
         cosq_ref, sinq_ref, wconv_ref, gv_ref, bv_ref, ms_ref, bs_ref,
         qlat_ref, qpe_ref, ckv_ref, kpe_ref, kcb_ref, kpb_ref, bconv_ref, xc_ref, cmix_ref, vn_ref,
         sga_ref, sgb_ref, sgc_ref, kct_ref, carry_ref) = refs

    x = x_ref[...]
    h = _rms_rows(x, gattn_ref[...]).astype(BF16)

    def proj(a, b):
        return _dot(h, wsm_ref[:, a:b])

    cq = proj(_OFF_CQ, _OFF_CKV)
    cqn = _rms_rows(cq, gq_ref[...]).astype(BF16)
    qall = _dot(cqn, wuq_ref[...])
    n_nope = N_HEADS_A * NOPE_DIM
    n_pe = N_HEADS_A * ROPE_DIM
    qpe = (qall[:, n_nope:n_nope + n_pe] * cosq_ref[...]
           + qall[:, n_nope + n_pe:n_nope + 2 * n_pe] * sinq_ref[...]) * MLA_SCALE
    qlat = _dot(qall[:, :n_nope].astype(BF16), wukbd_ref[...]) * MLA_SCALE
    if sample:
        for hd in range(N_HEADS_A):
            qlat_ref[hd] = qlat[:, hd * KV_RANK:(hd + 1) * KV_RANK]
            qpe_ref[hd] = qpe[:, hd * ROPE_DIM:(hd + 1) * ROPE_DIM]
    else:
        for blk in range(tm // Q_BLOCK):
            r0 = blk * Q_BLOCK
            qpe_t = qpe[r0:r0 + Q_BLOCK, :].T
            for hd in range(N_HEADS_A):
                cols = slice(hd * Q_BLOCK, (hd + 1) * Q_BLOCK)
                qlat_ref[blk, :, cols] = qlat[r0:r0 + Q_BLOCK, hd * KV_RANK:(hd + 1) * KV_RANK].T.astype(BF16)
                qpe_ref[blk, :, cols] = qpe_t[hd * ROPE_DIM:(hd + 1) * ROPE_DIM, :].astype(BF16)

    ckv = _rms_rows(proj(_OFF_CKV, _OFF_CB), gkv_ref[...])
    ckv_ref[...] = ckv
    kcb_ref[...] = ckv.astype(BF16)
    if not sample:
        kct_ref[...] = ckv.T.astype(BF16)
    kpe = proj(_OFF_KPE, _OFF_KPE_SW) * cosk_ref[...] + proj(_OFF_KPE_SW, _N_SMALL) * sink_ref[...]
    kpe_ref[...] = kpe
    kpb_ref[...] = kpe.astype(BF16)

    cb = proj(_OFF_CB, _OFF_CC)
    xc = proj(_OFF_CC, _OFF_CH) * proj(_OFF_CH, _OFF_U)
    xc_ref[...] = xc
    rows = lax.broadcasted_iota(jnp.int32, (tm, 1), 0)
    r1 = pltpu.roll(xc, 1, 0)
    r2 = pltpu.roll(xc, 2, 0)
    if sample:
        pos = rows % 8
        m1 = jnp.where(pos == 0, f1_ref[...], r1)
        m2 = jnp.where(pos < 2, f2_ref[...], r2)
    else:
        first = (pl.program_id(0) % blocks_per_seq) == 0
        prev = jnp.where(first, 0.0, carry_ref[...])
        p6 = prev[6:7, :]
        p7 = prev[7:8, :]
        m1 = jnp.where(rows == 0, p7, r1)
        m2 = jnp.where(rows == 0, p6, jnp.where(rows == 1, p7, r2))
        carry_ref[...] = xc[tm - 8:tm, :]
    y = m2 * wconv_ref[0:1, :] + m1 * wconv_ref[1:2, :] + xc * wconv_ref[2:3, :]
    bconv_ref[...] = (cb * y).astype(BF16)

    u = proj(_OFF_U, _OFF_V)
    v = proj(_OFF_V, _OFF_KPE)
    vc = v - jnp.mean(v, axis=-1, keepdims=True)
    vn = vc * lax.rsqrt(jnp.mean(vc * vc, axis=-1, keepdims=True) + EPS) * gv_ref[...] + bv_ref[...]
    vn_ref[...] = vn
    vnb = vn.astype(BF16)
    gw = W_C // N_GROUPS_C
    for r0 in range(0, tm, CHUNK):
        for g in range(N_GROUPS_C):
            mix = _dot(ms_ref[g], vnb[r0:r0 + CHUNK, g * gw:(g + 1) * gw]) + bs_ref[:, g * gw:(g + 1) * gw]
            cmix_ref[r0:r0 + CHUNK, g * gw:(g + 1) * gw] = (
                u[r0:r0 + CHUNK, g * gw:(g + 1) * gw] * mix).astype(BF16)

    sga_ref[...] = jax.nn.sigmoid(_dot(h, wg_ref[:, 0:d_model]))
    sgb_ref[...] = jax.nn.sigmoid(_dot(h, wg_ref[:, d_model:2 * d_model]))
    sgc_ref[...] = jax.nn.sigmoid(_dot(h, wg_ref[:, 2 * d_model:3 * d_model]))


def _inproj(x, lw, tabs, sample, seq_len, conv_fix, tm):
    t, d = x.shape
    nblk = t // tm
    blocks_per_seq = max(seq_len // tm, 1)
    row = lambda w: pl.BlockSpec((tm, w), lambda i: (i, 0))
    if sample:
        tab_spec = lambda w: pl.BlockSpec((tm, w), lambda i: (0, 0))
    else:
        tab_spec = lambda w: pl.BlockSpec((tm, w), lambda i: (i % blocks_per_seq, 0))
    n_pe = N_HEADS_A * ROPE_DIM
    in_specs = [
        row(d), _const_spec((1, d)), _const_spec(lw["w_small"].shape), _const_spec(lw["w_gates"].shape),
        _const_spec((1, Q_RANK)), _const_spec((1, KV_RANK)), _const_spec(lw["w_uq"].shape),
        _const_spec(lw["w_uk_bd"].shape),
        tab_spec(ROPE_DIM), tab_spec(ROPE_DIM), tab_spec(n_pe), tab_spec(n_pe),
        _const_spec((CONV_W, W_B)), _const_spec((1, W_C)), _const_spec((1, W_C)),
        _const_spec((N_GROUPS_C, CHUNK, CHUNK)), _const_spec((CHUNK, W_C)),
    ]
    args = [x, lw["g_attn"], lw["w_small"], lw["w_gates"], lw["g_q"], lw["g_kv"], lw["w_uq"], lw["w_uk_bd"],
            tabs["cosk"], tabs["sink"], tabs["cosq"], tabs["sinq"], lw["w_conv"], lw["g_v"], lw["b_v"],
            lw["ms_s"] if sample else lw["ms_p"], lw["bs_s"] if sample else lw["bs_p"]]
    scratch = []
    if sample:
        in_specs += [row(W_B), row(W_B)]
        args += [conv_fix[0], conv_fix[1]]
    else:
        scratch = [pltpu.VMEM((8, W_B), F32)]
    n_qcols = N_HEADS_A * Q_BLOCK
    if sample:
        q_shapes = [jax.ShapeDtypeStruct((N_HEADS_A, t, KV_RANK), F32),
                    jax.ShapeDtypeStruct((N_HEADS_A, t, ROPE_DIM), F32)]
        q_specs = [pl.BlockSpec((N_HEADS_A, tm, KV_RANK), lambda i: (0, i, 0)),
                   pl.BlockSpec((N_HEADS_A, tm, ROPE_DIM), lambda i: (0, i, 0))]
    else:
        q_shapes = [jax.ShapeDtypeStruct((t // Q_BLOCK, KV_RANK, n_qcols), BF16),
                    jax.ShapeDtypeStruct((t // Q_BLOCK, ROPE_DIM, n_qcols), BF16)]
        q_specs = [pl.BlockSpec((tm // Q_BLOCK, KV_RANK, n_qcols), lambda i: (i, 0, 0)),
                   pl.BlockSpec((tm // Q_BLOCK, ROPE_DIM, n_qcols), lambda i: (i, 0, 0))]
    out_shape = q_shapes + [
        jax.ShapeDtypeStruct((t, KV_RANK), F32),
        jax.ShapeDtypeStruct((t, ROPE_DIM), F32),
        jax.ShapeDtypeStruct((t, KV_RANK), BF16),
        jax.ShapeDtypeStruct((t, ROPE_DIM), BF16),
        jax.ShapeDtypeStruct((t, W_B), BF16),
        jax.ShapeDtypeStruct((t, W_B), F32),
        jax.ShapeDtypeStruct((t, W_C), BF16),
        jax.ShapeDtypeStruct((t, W_C), F32),
        jax.ShapeDtypeStruct((t, d), F32),
        jax.ShapeDtypeStruct((t, d), F32),
        jax.ShapeDtypeStruct((t, d), F32),
    ]
    out_specs = q_specs + [row(KV_RANK), row(ROPE_DIM), row(KV_RANK),
                           row(ROPE_DIM), row(W_B), row(W_B), row(W_C), row(W_C), row(d), row(d), row(d)]
    if not sample:
        out_shape.append(jax.ShapeDtypeStruct((KV_RANK, t), BF16))
        out_specs.append(pl.BlockSpec((KV_RANK, tm), lambda i: (0, i)))
    return pl.pallas_call(
        functools.partial(_inproj_body, sample, tm, blocks_per_seq, d),
        grid=(nblk,),
        in_specs=in_specs,
        out_specs=out_specs,
        out_shape=out_shape,
        scratch_shapes=scratch,
        compiler_params=_cparams(("arbitrary",)),
        name="inproj_sample" if sample else "inproj_prompt",
    )(*args)


def _softmax_step(s, k_bf, m_ref, l_ref, acc_ref):
    m_prev = m_ref[...]
    m_next = jnp.maximum(m_prev, jnp.max(s, axis=1, keepdims=True))
    p = jnp.exp(s - m_next)
    alpha = jnp.exp(m_prev - m_next)
    l_ref[...] = l_ref[...] * alpha + jnp.sum(p, axis=1, keepdims=True)
    acc_ref[...] = acc_ref[...] * alpha + _dot(p.astype(BF16), k_bf)
    m_ref[...] = m_next


ATTN_COL_GROUP = 1024


def _attn_prompt_body(kb, qlt_ref, qpt_ref, kc_ref, kp_ref, kct_ref, o_ref, m_ref, l_ref, acc_ref):
    qi = pl.program_id(1)
    kj = pl.program_id(2)
    cols_all = N_HEADS_A * Q_BLOCK
    last = ((qi + 1) * Q_BLOCK - 1) // kb

    @pl.when(kj == 0)
    def _():
        m_ref[...] = jnp.full((1, cols_all), NEG_BIG, F32)
        l_ref[...] = jnp.zeros((1, cols_all), F32)
        acc_ref[...] = jnp.zeros((KV_RANK, cols_all), F32)

    def step(masked):
        kc = kc_ref[...]
        kp = kp_ref[...]
        kct = kct_ref[...]
        for c0 in range(0, cols_all, ATTN_COL_GROUP):
            cols = slice(c0, c0 + ATTN_COL_GROUP)
            s = _dot(kc, qlt_ref[:, cols]) + _dot(kp, qpt_ref[:, cols])
            if masked:
                kpos = kj * kb + lax.broadcasted_iota(jnp.int32, (kb, 1), 0)
                qpos = qi * Q_BLOCK + lax.broadcasted_iota(jnp.int32, (1, ATTN_COL_GROUP), 1) % Q_BLOCK
                s = jnp.where(kpos <= qpos, s, NEG_BIG)
            m_prev = m_ref[:, cols]
            m_next = jnp.maximum(m_prev, jnp.max(s, axis=0, keepdims=True))
            p = jnp.exp(s - m_next)
            alpha = jnp.exp(m_prev - m_next)
            l_ref[:, cols] = l_ref[:, cols] * alpha + jnp.sum(p, axis=0, keepdims=True)
            acc_ref[:, cols] = acc_ref[:, cols] * alpha + _dot(kct, p.astype(BF16))
            m_ref[:, cols] = m_next

    @pl.when(kj < last)
    def _():
        step(False)

    @pl.when(kj == last)
    def _():
        step(True)
        o_t = acc_ref[...] / l_ref[...]
        for hd in range(N_HEADS_A):
            o_ref[hd] = o_t[:, hd * Q_BLOCK:(hd + 1) * Q_BLOCK].T.astype(BF16)


def _attn_prompt(qlt, qpt, kcb, kpb, kct, batch, seq, kb):
    nq = seq // Q_BLOCK
    nk = seq // kb
    t = batch * seq
    cols_all = N_HEADS_A * Q_BLOCK

    def kblk(b, qi, kj):
        return b * nk + jnp.minimum(kj, ((qi + 1) * Q_BLOCK - 1) // kb)

    qmap = lambda b, qi, kj: (b * nq + qi, 0, 0)
    return pl.pallas_call(
        functools.partial(_attn_prompt_body, kb),
        grid=(batch, nq, nk),
        in_specs=[
            pl.BlockSpec((None, KV_RANK, cols_all), qmap),
            pl.BlockSpec((None, ROPE_DIM, cols_all), qmap),
            pl.BlockSpec((kb, KV_RANK), lambda b, qi, kj: (kblk(b, qi, kj), 0)),
            pl.BlockSpec((kb, ROPE_DIM), lambda b, qi, kj: (kblk(b, qi, kj), 0)),
            pl.BlockSpec((KV_RANK, kb), lambda b, qi, kj: (0, kblk(b, qi, kj))),
        ],
        out_specs=pl.BlockSpec((N_HEADS_A, Q_BLOCK, KV_RANK), lambda b, qi, kj: (0, b * nq + qi, 0)),
        out_shape=jax.ShapeDtypeStruct((N_HEADS_A, t, KV_RANK), BF16),
        scratch_shapes=[pltpu.VMEM((1, cols_all), F32), pltpu.VMEM((1, cols_all), F32),
                        pltpu.VMEM((KV_RANK, cols_all), F32)],
        compiler_params=_cparams(("arbitrary", "arbitrary", "arbitrary")),
        name="attn_prompt",
    )(qlt, qpt, kcb, kpb, kct)


def _attn_sample_body(pp, dec_seq, page, *refs):
    pt_ref = refs[0]
    del pt_ref
    qlat_ref, qpe_ref = refs[1], refs[2]
    c_refs = refs[3:3 + pp]
    p_refs = refs[3 + pp:3 + 2 * pp]
    knc_ref, knp_ref = refs[3 + 2 * pp], refs[4 + 2 * pp]
    o_ref = refs[5 + 2 * pp]
    kbuf_ref, pbuf_ref, m_ref, l_ref, acc_ref = refs[6 + 2 * pp:]
    j = pl.program_id(1)
    nj = pl.num_programs(1)
    rows = N_HEADS_A * dec_seq

    @pl.when(j == 0)
    def _():
        m_ref[...] = jnp.full((rows, 1), NEG_BIG, F32)
        l_ref[...] = jnp.zeros((rows, 1), F32)
        acc_ref[...] = jnp.zeros((rows, KV_RANK), F32)

    q = qlat_ref[...].reshape(rows, KV_RANK).astype(BF16)
    qp = qpe_ref[...].reshape(rows, ROPE_DIM).astype(BF16)
    for p in range(pp):
        kbuf_ref[p * page:(p + 1) * page, :] = c_refs[p][...].astype(BF16)
        pbuf_ref[:, p * page:(p + 1) * page] = p_refs[p][...].astype(BF16)
    k = kbuf_ref[...]
    s = _dot_nt(q, k) + _dot(qp, pbuf_ref[...])
    _softmax_step(s, k, m_ref, l_ref, acc_ref)

    @pl.when(j == nj - 1)
    def _():
        kn = knc_ref[...]
        sn = _dot_nt(q, kn) + _dot(qp, knp_ref[...])
        qpos = lax.broadcasted_iota(jnp.int32, (rows, 1), 0) % dec_seq
        kpos = lax.broadcasted_iota(jnp.int32, (1, page), 1)
        sn = jnp.where(kpos <= qpos, sn, NEG_BIG)
        _softmax_step(sn, kn, m_ref, l_ref, acc_ref)
        o = acc_ref[...] / l_ref[...]
        o_ref[...] = o.reshape(N_HEADS_A, dec_seq, KV_RANK)


def _attn_sample(layer, qlat, qpe, cache_ckv, cache_kpe, page_table, knew_c, knew_p, dec_seq, pp):
    dec_b, n_pages = page_table.shape
    page = cache_ckv.shape[2]
    t = dec_b * dec_seq
    qmap = lambda b, j, pt: (0, b, 0)

    def cmap(p):
        return lambda b, j, pt: (layer, pt[b, j * pp + p], 0, 0)

    in_specs = [pl.BlockSpec((N_HEADS_A, dec_seq, KV_RANK), qmap),
                pl.BlockSpec((N_HEADS_A, dec_seq, ROPE_DIM), qmap)]
    in_specs += [pl.BlockSpec((None, None, page, KV_RANK), cmap(p)) for p in range(pp)]
    in_specs += [pl.BlockSpec((None, None, ROPE_DIM, page), cmap(p)) for p in range(pp)]
    in_specs += [pl.BlockSpec((None, page, KV_RANK), lambda b, j, pt: (b, 0, 0)),
                 pl.BlockSpec((None, ROPE_DIM, page), lambda b, j, pt: (b, 0, 0))]
    rows = N_HEADS_A * dec_seq
    grid_spec = pltpu.PrefetchScalarGridSpec(
        num_scalar_prefetch=1,
        grid=(dec_b, n_pages // pp),
        in_specs=in_specs,
        out_specs=pl.BlockSpec((N_HEADS_A, dec_seq, KV_RANK), qmap),
        scratch_shapes=[pltpu.VMEM((pp * page, KV_RANK), BF16), pltpu.VMEM((ROPE_DIM, pp * page), BF16),
                        pltpu.VMEM((rows, 1), F32), pltpu.VMEM((rows, 1), F32),
                        pltpu.VMEM((rows, KV_RANK), F32)],
    )
    return pl.pallas_call(
        functools.partial(_attn_sample_body, pp, dec_seq, page),
        grid_spec=grid_spec,
        out_shape=jax.ShapeDtypeStruct((N_HEADS_A, t, KV_RANK), F32),
        compiler_params=_cparams(("arbitrary", "arbitrary")),
        name="attn_sample",
    )(page_table, qlat, qpe, *([cache_ckv] * pp), *([cache_kpe] * pp), knew_c, knew_p)


def _merge_body(x_ref, olat_ref, wuvbd_ref, bconv_ref, cmix_ref, sga_ref, sgb_ref, sgc_ref,
                wbra_ref, wbrb_ref, wbrc_ref, wout_ref, gffn_ref, wpq_ref, skeys_ref,
                x1_ref, h2t_ref, st_ref):
    ocat = jnp.concatenate([olat_ref[hd].astype(BF16) for hd in range(N_HEADS_A)], axis=1)
    a = _dot(ocat, wuvbd_ref[...]).astype(BF16)
    merged = (sga_ref[...] * _dot(a, wbra_ref[...])
              + sgb_ref[...] * _dot(bconv_ref[...], wbrb_ref[...])
              + sgc_ref[...] * _dot(cmix_ref[...], wbrc_ref[...]))
    x1 = x_ref[...] + _dot(merged.astype(BF16), wout_ref[...])
    x1_ref[...] = x1
    h2 = _rms_rows(x1, gffn_ref[...])
    h2t_ref[...] = h2.T.astype(BF16)
    q = _dot(h2.astype(BF16), wpq_ref[...]).astype(BF16)
    half = q.shape[1] // (2 * N_HEADS_P)
    for hc in range(2 * N_HEADS_P):
        st_ref[hc * N_KEYS:(hc + 1) * N_KEYS, :] = _dot_nt(skeys_ref[hc], q[:, hc * half:(hc + 1) * half])


def _merge(x, olat, bconv, cmix, sga, sgb, sgc, lw, tm):
    t, d = x.shape
    row = lambda w: pl.BlockSpec((tm, w), lambda i: (i, 0))
    col = lambda r: pl.BlockSpec((r, tm), lambda i: (0, i))
    n_s = 2 * N_HEADS_P * N_KEYS
    return pl.pallas_call(
        _merge_body,
        grid=(t // tm,),
        in_specs=[row(d), pl.BlockSpec((N_HEADS_A, tm, KV_RANK), lambda i: (0, i, 0)),
                  _const_spec(lw["w_uv_bd"].shape), row(W_B), row(W_C), row(d), row(d), row(d),
                  _const_spec(lw["w_br_a"].shape), _const_spec(lw["w_br_b"].shape),
                  _const_spec(lw["w_br_c"].shape), _const_spec(lw["w_out"].shape), _const_spec((1, d)),
                  _const_spec(lw["w_pq"].shape), _const_spec(lw["skeys"].shape)],
        out_specs=[row(d), col(d), col(n_s)],
        out_shape=[jax.ShapeDtypeStruct((t, d), F32), jax.ShapeDtypeStruct((d, t), BF16),
                   jax.ShapeDtypeStruct((n_s, t), F32)],
        compiler_params=_cparams(("arbitrary",)),
        name="merge",
    )(x, olat, lw["w_uv_bd"], bconv, cmix, sga, sgb, sgc, lw["w_br_a"], lw["w_br_b"], lw["w_br_c"],
      lw["w_out"], lw["g_ffn"], lw["w_pq"], lw["skeys"])


def _top_values(cur, n, out_ref, base):
    for k in range(n):
        mx = jnp.max(cur, axis=0, keepdims=True)
        out_ref[base + k:base + k + 1, :] = mx
        if k + 1 < n:
            cur = jnp.where(cur == mx, -jnp.inf, cur)


def _peer_select_body(st_ref, a0_ref, b1_ref, cthr_ref, top_ref, cand_ref):
    tn = st_ref.shape[1]
    n_half = N_HEADS_P * N_KEYS
    for hc in range(2 * N_HEADS_P):
        _top_values(st_ref[hc * N_KEYS:(hc + 1) * N_KEYS, :], TOPK_P, top_ref, hc * TOPK_P)
    rank = lax.broadcasted_iota(jnp.int32, (TOPK_P, 1), 0)
    for hd in range(N_HEADS_P):
        s0 = top_ref[hd * TOPK_P:(hd + 1) * TOPK_P, :]
        s1 = top_ref[(N_HEADS_P + hd) * TOPK_P:(N_HEADS_P + hd + 1) * TOPK_P, :]
        half = TOPK_P // 2
        s0_lo = s0[0:half, :]
        part0 = s0 + s1[0:1, :]
        mid = [jnp.where(rank[0:half] < TOPK_P // (b + 1), s0_lo + s1[b:b + 1, :], -jnp.inf)
               for b in range(1, half)]
        tail = s0[0:1, :] + s1[half:TOPK_P, :]
        cands = jnp.concatenate([part0] + mid + [tail], axis=0)
        _top_values(cands, TOPK_P, cand_ref, 0)
        thr = cand_ref[TOPK_P - 1:TOPK_P, :]
        mx = cand_ref[0:1, :]
        z = jnp.sum(jnp.where(cands >= thr, jnp.exp(cands - mx), 0.0), axis=0, keepdims=True)
        tau_lo = jnp.full((half, tn), jnp.inf, F32)
        for b in range(1, half):
            tau_lo = jnp.minimum(tau_lo, jnp.where(mid[b - 1] >= thr, s1[b:b + 1, :], jnp.inf))
        tau_0 = jnp.min(jnp.where(tail >= thr, s1[half:TOPK_P, :], jnp.inf), axis=0, keepdims=True)
        tau_lo = jnp.minimum(tau_lo, jnp.where(rank[0:half] == 0, tau_0, jnp.inf))
        tau = jnp.minimum(jnp.where(part0 >= thr, s1[0:1, :], jnp.inf),
                          jnp.concatenate([tau_lo, jnp.full((half, tn), jnp.inf, F32)], axis=0))
        s0_all = st_ref[hd * N_KEYS:(hd + 1) * N_KEYS, :]
        s1_all = st_ref[n_half + hd * N_KEYS:n_half + (hd + 1) * N_KEYS, :]

        def factor1(x):
            return jnp.exp(x - s1[0:1, :]) * (0.5 / z)

        beta = factor1(tau)
        cthr = jnp.full((N_KEYS, tn), jnp.inf, F32)
        for a in reversed(range(TOPK_P)):
            cthr = jnp.where(s0_all >= s0[a:a + 1, :], beta[a:a + 1, :], cthr)
        cthr_ref[hd * N_KEYS:(hd + 1) * N_KEYS, :] = cthr
        a0_ref[hd * N_KEYS:(hd + 1) * N_KEYS, :] = jnp.exp(s0_all - s0[0:1, :])
        b1_ref[hd * N_KEYS:(hd + 1) * N_KEYS, :] = factor1(s1_all)


def _peer_select(st, tn):
    n_s, t = st.shape
    n_h = N_HEADS_P * N_KEYS
    col = lambda r: pl.BlockSpec((r, tn), lambda i: (0, i))
    return pl.pallas_call(
        _peer_select_body,
        grid=(t // tn,),
        in_specs=[col(n_s)],
        out_specs=[col(n_h), col(n_h), col(n_h)],
        out_shape=[jax.ShapeDtypeStruct((n_h, t), F32)] * 3,
        scratch_shapes=[pltpu.VMEM((2 * N_HEADS_P * TOPK_P, tn), F32), pltpu.VMEM((TOPK_P, tn), F32)],
        compiler_params=_cparams(("arbitrary",)),
        name="peer_select",
    )(st)


def _peer_gate_block(item, at, a0_ref, b1_ref, cthr_ref, w_ref, keys_per_block, lane_chunk):
    tn = at.shape[1]
    for c0 in range(0, tn, lane_chunk):
        lanes = slice(c0, c0 + lane_chunk)
        for ii in range(keys_per_block):
            gate = None
            for hd in range(N_HEADS_P):
                b1 = b1_ref[hd * N_KEYS:(hd + 1) * N_KEYS, lanes]
                cthr = cthr_ref[hd, item, ii:ii + 1, lanes]
                a0 = a0_ref[hd, item, ii:ii + 1, lanes]
                term = a0 * jnp.where(b1 >= cthr, b1, 0.0)
                gate = term if gate is None else gate + term
            xa = at[ii * N_KEYS:(ii + 1) * N_KEYS, lanes]
            act2 = xa * (1.0 + lax.erf(xa * (2.0 ** -0.5)))
            w_ref[ii * N_KEYS:(ii + 1) * N_KEYS, lanes] = (gate * act2).astype(BF16)


def _peer_dense_body(eb, lane_chunk, h2t_ref, u_ref, vta_ref, vtb_ref, vtc_ref, a0_ref, b1_ref,
                     cthr_ref, out_ref, w0_ref, w1_ref):
    k = pl.program_id(1)
    nk = pl.num_programs(1)
    keys_per_block = eb // N_KEYS
    gate_args = (a0_ref, b1_ref, cthr_ref)

    @pl.when(k == 0)
    def _():
        out_ref[...] = jnp.zeros(out_ref.shape, F32)
        w1_ref[...] = jnp.zeros(w1_ref.shape, BF16)

    h2t = h2t_ref[...]
    at0 = _dot(u_ref[0:eb, :], h2t)
    out_ref[...] += _dot(vta_ref[...], w1_ref[...])
    _peer_gate_block(0, at0, *gate_args, w0_ref, keys_per_block, lane_chunk)
    at1 = _dot(u_ref[eb:2 * eb, :], h2t)
    out_ref[...] += _dot(vtb_ref[...], w0_ref[...])
    _peer_gate_block(1, at1, *gate_args, w1_ref, keys_per_block, lane_chunk)

    @pl.when(k == nk - 1)
    def _():
        out_ref[...] += _dot(vtc_ref[...], w1_ref[...])


def _peer_dense(h2t, u_bf, vt_bf, a0, b1, cthr, tn, eb):
    d, t = h2t.shape
    n_e = u_bf.shape[0]
    n_blocks = n_e // eb
    assert eb == 8 * N_KEYS
    assert n_blocks % 2 == 0
    col = lambda r: pl.BlockSpec((r, tn), lambda i, k: (0, i))
    n_h = N_HEADS_P * N_KEYS
    keys_per_block = eb // N_KEYS
    by_block = lambda a: a.reshape(N_HEADS_P, n_blocks, keys_per_block, t)
    blk_rows = pl.BlockSpec((N_HEADS_P, 2, keys_per_block, tn), lambda i, k: (0, k, 0, i))
    return pl.pallas_call(
        functools.partial(_peer_dense_body, eb, 128),
        grid=(t // tn, n_blocks // 2),
        in_specs=[col(d), pl.BlockSpec((2 * eb, d), lambda i, k: (k, 0)),
                  pl.BlockSpec((d, eb), lambda i, k: (0, jnp.maximum(2 * k - 1, 0))),
                  pl.BlockSpec((d, eb), lambda i, k: (0, 2 * k)),
                  pl.BlockSpec((d, eb), lambda i, k: (0, n_blocks - 1)),
                  blk_rows, col(n_h), blk_rows],
        out_specs=col(d),
        out_shape=jax.ShapeDtypeStruct((d, t), F32),
        scratch_shapes=[pltpu.VMEM((eb, tn), BF16), pltpu.VMEM((eb, tn), BF16)],
        compiler_params=_cparams(("arbitrary", "arbitrary")),
        name="peer_dense",
    )(h2t, u_bf, vt_bf, vt_bf, vt_bf, by_block(a0), b1, by_block(cthr))


def _table_prep_body(u_ref, v_ref, ub_ref, vt_ref):
    ub_ref[...] = u_ref[...].astype(BF16)
    vt_ref[...] = v_ref[...].T.astype(BF16)


def _table_prep(layer, u_tab, v_tab, rows):
    _, n_e, d = u_tab.shape
    src = pl.BlockSpec((None, rows, d), lambda e: (layer, e, 0))
    return pl.pallas_call(
        _table_prep_body,
        grid=(n_e // rows,),
        in_specs=[src, src],
        out_specs=[pl.BlockSpec((rows, d), lambda e: (e, 0)), pl.BlockSpec((d, rows), lambda e: (0, e))],
        out_shape=[jax.ShapeDtypeStruct((n_e, d), BF16), jax.ShapeDtypeStruct((d, n_e), BF16)],
        compiler_params=_cparams(("arbitrary",)),
        name="table_prep",
    )(u_tab, v_tab)


def _ple_body(last, x1_ref, pt_ref, gple_ref, wpg_ref, p_ref, wple_ref, *rest):
    if last:
        gfin_ref, x3_ref, y_ref = rest
    else:
        (x3_ref,) = rest
    x2 = x1_ref[...] + pt_ref[...].T
    hp = _rms_rows(x2, gple_ref[...]).astype(BF16)
    gate = jax.nn.sigmoid(_dot(hp, wpg_ref[...]))
    x3 = x2 + gate * _dot(p_ref[...].astype(BF16), wple_ref[...])
    x3_ref[...] = x3
    if last:
        y_ref[...] = _rms_rows(x3, gfin_ref[...])


def _ple(x1, peer_t, col_off, p_i, lw, g_final, last, tm):
    t, d = x1.shape
    row = lambda w: pl.BlockSpec((tm, w), lambda i: (i, 0))
    in_specs = [row(d), pl.BlockSpec((d, tm), lambda i: (0, i + col_off)), _const_spec((1, d)),
                _const_spec(lw["w_pg"].shape), row(p_i.shape[1]), _const_spec(lw["w_ple"].shape)]
    args = [x1, peer_t, lw["g_ple"], lw["w_pg"], p_i, lw["w_ple"]]
    out_specs = [row(d)]
    out_shape = [jax.ShapeDtypeStruct((t, d), F32)]
    if last:
        in_specs.append(_const_spec((1, d)))
        args.append(g_final)
        out_specs.append(row(d))
        out_shape.append(jax.ShapeDtypeStruct((t, d), F32))
    return pl.pallas_call(
        functools.partial(_ple_body, last),
        grid=(t // tm,),
        in_specs=in_specs,
        out_specs=out_specs,
        out_shape=out_shape,
        compiler_params=_cparams(("arbitrary",)),
        name="ple_last" if last else "ple",
    )(*args)


def _rope_tables(pos):
    inv = 1.0 / (ROPE_THETA ** (jnp.arange(0, ROPE_DIM, 2, dtype=F32) / ROPE_DIM))
    ang = pos.astype(F32)[:, None] * inv[None, :]
    cos, sin = jnp.cos(ang), jnp.sin(ang)
    cosk = jnp.concatenate([cos, cos], axis=1)
    sink = jnp.concatenate([-sin, sin], axis=1)
    return {"cosk": cosk, "sink": sink, "cosq": jnp.tile(cosk, (1, N_HEADS_A)),
            "sinq": jnp.tile(sink, (1, N_HEADS_A))}


def _swap_halves(w):
    half = w.shape[-1] // 2
    return jnp.concatenate([w[..., half:], w[..., :half]], axis=-1)


def _block_diag(blocks):
    n, r, c = blocks.shape
    eye = jnp.eye(n, dtype=blocks.dtype)
    return (eye[:, None, :, None] * blocks[:, :, None, :]).reshape(n * r, n * c)


def _layer_weights(i, d, dec_seq, g_attn, w_in, g_q, g_kv, w_uq, w_uk, w_uv, w_conv, g_v, b_v, w_s, b_s,
                   w_br_a, w_br_b, w_br_c, w_out, g_ffn, w_pq, sub_keys, u_tab, v_tab, g_ple, w_pg, w_ple):
    bf = lambda a: a.astype(BF16)
    wi = w_in[i]
    o_kpe = Q_RANK + KV_RANK
    o_cb = o_kpe + ROPE_DIM
    o_g = o_cb + 3 * W_B + 2 * W_C
    w_kpe = wi[:, o_kpe:o_cb]
    w_small = jnp.concatenate([wi[:, :o_kpe], wi[:, o_cb:o_g], w_kpe, _swap_halves(w_kpe)], axis=1)
    uq = w_uq[i]
    uq_pe = uq[:, :, NOPE_DIM:]
    w_uq_p = jnp.concatenate([uq[:, :, :NOPE_DIM].reshape(Q_RANK, -1), uq_pe.reshape(Q_RANK, -1),
                              _swap_halves(uq_pe).reshape(Q_RANK, -1)], axis=1)
    ws = jnp.tril(w_s[i])
    ws_small = jnp.tril(w_s[i][:, :dec_seq, :dec_seq])
    eye = jnp.eye(CHUNK // dec_seq, dtype=F32)
    ms_s = jnp.einsum("ab,gts->gatbs", eye, ws_small).reshape(N_GROUPS_C, CHUNK, CHUNK)
    gw = W_C // N_GROUPS_C
    bs_p = jnp.repeat(b_s[i].T, gw, axis=1)
    bs_s = jnp.tile(bs_p[:dec_seq], (CHUNK // dec_seq, 1))
    return {
        "g_attn": g_attn[i][None], "w_small": bf(w_small), "w_gates": bf(wi[:, o_g:]),
        "g_q": g_q[i][None], "g_kv": g_kv[i][None], "w_uq": bf(w_uq_p),
        "w_uk_bd": bf(_block_diag(jnp.transpose(w_uk[i], (1, 2, 0)))),
        "w_uv_bd": bf(_block_diag(jnp.transpose(w_uv[i], (1, 0, 2)))),
        "w_conv": w_conv[i], "g_v": g_v[i][None], "b_v": b_v[i][None],
        "ms_p": bf(ws), "ms_s": bf(ms_s), "bs_p": bs_p, "bs_s": bs_s,
        "w_br_a": bf(w_br_a[i]), "w_br_b": bf(w_br_b[i]), "w_br_c": bf(w_br_c[i]), "w_out": bf(w_out[i]),
        "g_ffn": g_ffn[i][None], "w_pq": bf(jnp.transpose(w_pq[i], (0, 2, 1, 3)).reshape(d, -1)),
        "skeys": bf(jnp.transpose(sub_keys[i], (1, 0, 2, 3)).reshape(2 * N_HEADS_P, N_KEYS, -1)),
        "g_ple": g_ple[i][None], "w_pg": bf(w_pg[i]), "w_ple": bf(w_ple[i]),
    }


def _pick_tile(n, prefs):
    for p in prefs:
        if n % p == 0:
            return p
    raise ValueError(f"no tile for {n}")


def kernel(x_prompt, x_sample, p_prompt, p_sample, cache_ckv, cache_kpe, state_conv, page_table,
           g_attn, w_in, g_q, g_kv, w_uq, w_uk, w_uv, w_conv, g_v, b_v, w_s, b_s,
           w_br_a, w_br_b, w_br_c, w_out, g_ffn, w_pq, sub_keys, u_tab, v_tab,
           g_ple, w_pg, w_ple, g_final):
    batch, seq, d = x_prompt.shape
    dec_b, dec_seq, _ = x_sample.shape
    depth = w_in.shape[0]
    n_pages = page_table.shape[1]
    page = cache_ckv.shape[2]
    past_len = n_pages * page
    tp, ts = batch * seq, dec_b * dec_seq
    assert seq % CHUNK == 0 and CHUNK % dec_seq == 0 and ts % CHUNK == 0 and dec_seq == 8
    assert page == CHUNK

    tm_p = _pick_tile(seq, (256, 128))
    tm_s = _pick_tile(ts, (256, 128))
    kb = _pick_tile(seq, (512, 256, 128))
    pp = _pick_tile(n_pages, (16, 8, 4, 2, 1))
    tn_p = _pick_tile(tp, (512, 256, 128))
    tn_s = _pick_tile(ts, (512, 256, 128))
    eb = 1024

    tabs_p = _rope_tables(jnp.arange(seq, dtype=jnp.int32))
    tabs_s = _rope_tables(past_len + (jnp.arange(tm_s, dtype=jnp.int32) % dec_seq))
    g_fin = g_final[None]
    cache_kpe_t = jnp.swapaxes(cache_kpe, 2, 3)

    xp = x_prompt.reshape(tp, d)
    xs = x_sample.reshape(ts, d)
    outs = {k: [] for k in ("ckv_p", "kpe_p", "ckv_s", "kpe_s", "conv_p", "conv_s", "v_s")}
    y_p = y_s = None
    for i in range(depth):
        lw = _layer_weights(i, d, dec_seq, g_attn, w_in, g_q, g_kv, w_uq, w_uk, w_uv, w_conv, g_v, b_v,
                            w_s, b_s, w_br_a, w_br_b, w_br_c, w_out, g_ffn, w_pq, sub_keys, u_tab, v_tab,
                            g_ple, w_pg, w_ple)
        last = i == depth - 1

        (qlt_p, qpt_p, ckv_p, kpe_p, kcb_p, kpb_p, bconv_p, xc_p, cmix_p, _vn_p, sga_p, sgb_p, sgc_p,
         kct_p) = _inproj(xp, lw, tabs_p, False, seq, None, tm_p)
        olat_p = _attn_prompt(qlt_p, qpt_p, kcb_p, kpb_p, kct_p, batch, seq, kb)
        x1_p, h2t_p, st_p = _merge(xp, olat_p, bconv_p, cmix_p, sga_p, sgb_p, sgc_p, lw, tm_p)

        st_c = state_conv[i]
        zero_row = jnp.zeros((dec_b, 1, W_B), F32)
        f1 = jnp.concatenate([st_c[:, 1:2], jnp.zeros((dec_b, dec_seq - 1, W_B), F32)], axis=1).reshape(ts, W_B)
        f2 = jnp.concatenate([st_c[:, 0:1], st_c[:, 1:2], jnp.zeros((dec_b, dec_seq - 2, W_B), F32)],
                             axis=1).reshape(ts, W_B)
        del zero_row
        (qlat_s, qpe_s, ckv_s, kpe_s, kcb_s, kpb_s, bconv_s, xc_s, cmix_s, vn_s, sga_s, sgb_s, sgc_s) = _inproj(
            xs, lw, tabs_s, True, dec_seq, (f1, f2), tm_s)
        knew_c = jnp.pad(kcb_s.reshape(dec_b, dec_seq, KV_RANK), ((0, 0), (0, page - dec_seq), (0, 0)))
        knew_p = jnp.pad(jnp.swapaxes(kpb_s.reshape(dec_b, dec_seq, ROPE_DIM), 1, 2),
                         ((0, 0), (0, 0), (0, page - dec_seq)))
        olat_s = _attn_sample(i, qlat_s, qpe_s, cache_ckv, cache_kpe_t, page_table, knew_c, knew_p, dec_seq, pp)
        x1_s, h2t_s, st_s = _merge(xs, olat_s, bconv_s, cmix_s, sga_s, sgb_s, sgc_s, lw, tm_s)

        u_bf, vt_bf = _table_prep(i, u_tab, v_tab, eb)
        peer_p = _peer_dense(h2t_p, u_bf, vt_bf, *_peer_select(st_p, tn_p), tn_p, eb)
        peer_s = _peer_dense(h2t_s, u_bf, vt_bf, *_peer_select(st_s, tn_s), tn_s, eb)

        res_p = _ple(x1_p, peer_p, 0, p_prompt[i].reshape(tp, -1), lw, g_fin, last, tm_p)
        res_s = _ple(x1_s, peer_s, 0, p_sample[i].reshape(ts, -1), lw, g_fin, last, tm_s)
        xp, xs = res_p[0], res_s[0]
        if last:
            y_p, y_s = res_p[1], res_s[1]

        outs["ckv_p"].append(ckv_p.reshape(batch, seq, KV_RANK))
        outs["kpe_p"].append(kpe_p.reshape(batch, seq, ROPE_DIM))
        outs["ckv_s"].append(ckv_s.reshape(dec_b, dec_seq, KV_RANK))
        outs["kpe_s"].append(kpe_s.reshape(dec_b, dec_seq, ROPE_DIM))
        outs["conv_p"].append(xc_p.reshape(batch, seq, W_B)[:, seq - (CONV_W - 1):])
        outs["conv_s"].append(xc_s.reshape(dec_b, dec_seq, W_B)[:, dec_seq - (CONV_W - 1):])
        outs["v_s"].append(vn_s.reshape(dec_b, dec_seq, W_C))

    return (y_p.reshape(batch, seq, d), y_s.reshape(dec_b, dec_seq, d),
            jnp.stack(outs["ckv_p"]), jnp.stack(outs["kpe_p"]), jnp.stack(outs["ckv_s"]),
            jnp.stack(outs["kpe_s"]), jnp.stack(outs["conv_p"]), jnp.stack(outs["conv_s"]),
            jnp.stack(outs["v_s"]))
```

```python
import functools

import jax
import jax.numpy as jnp
from jax import lax
from jax.experimental import pallas as pl
from jax.experimental.pallas import tpu as pltpu

F32 = jnp.float32
BF16 = jnp.bfloat16

N_HEADS_A = 8
NOPE_DIM = 64
ROPE_DIM = 32
V_DIM = 64
Q_RANK = 768
KV_RANK = 256
ROPE_THETA = 10000.0
Q_BLOCK = 128
MLA_SCALE = (NOPE_DIM + ROPE_DIM) ** -0.5
W_B = 512
CONV_W = 3
W_C = 512
N_GROUPS_C = 4
CHUNK = 128
N_KEYS = 128
N_HEADS_P = 8
TOPK_P = 16
EPS = 1e-6

NEG_BIG = -1e30
VMEM_LIMIT_BYTES = 56 * 1024 * 1024

_OFF_CQ = 0
_OFF_CKV = _OFF_CQ + Q_RANK
_OFF_CB = _OFF_CKV + KV_RANK
_OFF_CC = _OFF_CB + W_B
_OFF_CH = _OFF_CC + W_B
_OFF_U = _OFF_CH + W_B
_OFF_V = _OFF_U + W_C
_OFF_KPE = _OFF_V + W_C
_OFF_KPE_SW = _OFF_KPE + ROPE_DIM
_N_SMALL = _OFF_KPE_SW + ROPE_DIM


def _cparams(semantics):
    return pltpu.CompilerParams(dimension_semantics=semantics, vmem_limit_bytes=VMEM_LIMIT_BYTES)


def _rms_rows(x, g):
    return x * lax.rsqrt(jnp.mean(x * x, axis=-1, keepdims=True) + EPS) * g


def _dot(a, b):
    return jnp.dot(a, b, preferred_element_type=F32)


def _dot_nt(a, b):
    return lax.dot_general(a, b, (((1,), (1,)), ((), ())), preferred_element_type=F32)


def _const_spec(shape):
    nd = len(shape)
    return pl.BlockSpec(shape, lambda *_: (0,) * nd)


def _inproj_body(sample, tm, blocks_per_seq, d_model, *refs):
    if sample:
        (x_ref, gattn_ref, wsm_ref, wg_ref, gq_ref, gkv_ref, wuq_ref, wukbd_ref, cosk_ref, sink_ref,
         cosq_ref, sinq_ref, wconv_ref, gv_ref, bv_ref, ms_ref, bs_ref, f1_ref, f2_ref,
         qlat_ref, qpe_ref, ckv_ref, kpe_ref, kcb_ref, kpb_ref, bconv_ref, xc_ref, cmix_ref, vn_ref,
         sga_ref, sgb_ref, sgc_ref) = refs
        carry_ref = kct_ref = None
    else:
        (x_ref, gattn_ref, wsm_ref, wg_ref, gq_ref, gkv_ref, wuq_ref, wukbd_ref, cosk_ref, sink_ref,
         cosq_ref, sinq_ref, wconv_ref, gv_ref, bv_ref, ms_ref, bs_ref,
         qlat_ref, qpe_ref, ckv_ref, kpe_ref, kcb_ref, kpb_ref, bconv_ref, xc_ref, cmix_ref, vn_ref,
         sga_ref, sgb_ref, sgc_ref, kct_ref, carry_ref) = refs

    x = x_ref[...]
    h = _rms_rows(x, gattn_ref[...]).astype(BF16)

    def proj(a, b):
        return _dot(h, wsm_ref[:, a:b])

    cq = proj(_OFF_CQ, _OFF_CKV)
    cqn = _rms_rows(cq, gq_ref[...]).astype(BF16)
    qall = _dot(cqn, wuq_ref[...])
    n_nope = N_HEADS_A * NOPE_DIM
    n_pe = N_HEADS_A * ROPE_DIM
    qpe = (qall[:, n_nope:n_nope + n_pe] * cosq_ref[...]
           + qall[:, n_nope + n_pe:n_nope + 2 * n_pe] * sinq_ref[...]) * MLA_SCALE
    qlat = _dot(qall[:, :n_nope].astype(BF16), wukbd_ref[...]) * MLA_SCALE
    if sample:
        for hd in range(N_HEADS_A):
            qlat_ref[hd] = qlat[:, hd * KV_RANK:(hd + 1) * KV_RANK]
            qpe_ref[hd] = qpe[:, hd * ROPE_DIM:(hd + 1) * ROPE_DIM]
    else:
        for blk in range(tm // Q_BLOCK):
            r0 = blk * Q_BLOCK
            qpe_t = qpe[r0:r0 + Q_BLOCK, :].T
            for hd in range(N_HEADS_A):
                cols = slice(hd * Q_BLOCK, (hd + 1) * Q_BLOCK)
                qlat_ref[blk, :, cols] = qlat[r0:r0 + Q_BLOCK, hd * KV_RANK:(hd + 1) * KV_RANK].T.astype(BF16)
                qpe_ref[blk, :, cols] = qpe_t[hd * ROPE_DIM:(hd + 1) * ROPE_DIM, :].astype(BF16)

    ckv = _rms_rows(proj(_OFF_CKV, _OFF_CB), gkv_ref[...])
    ckv_ref[...] = ckv
    kcb_ref[...] = ckv.astype(BF16)
    if not sample:
        kct_ref[...] = ckv.T.astype(BF16)
    kpe = proj(_OFF_KPE, _OFF_KPE_SW) * cosk_ref[...] + proj(_OFF_KPE_SW, _N_SMALL) * sink_ref[...]
    kpe_ref[...] = kpe
    kpb_ref[...] = kpe.astype(BF16)

    cb = proj(_OFF_CB, _OFF_CC)
    xc = proj(_OFF_CC, _OFF_CH) * proj(_OFF_CH, _OFF_U)
    xc_ref[...] = xc
    rows = lax.broadcasted_iota(jnp.int32, (tm, 1), 0)
    r1 = pltpu.roll(xc, 1, 0)
    r2 = pltpu.roll(xc, 2, 0)
    if sample:
        pos = rows % 8
        m1 = jnp.where(pos == 0, f1_ref[...], r1)
        m2 = jnp.where(pos < 2, f2_ref[...], r2)
    else:
        first = (pl.program_id(0) % blocks_per_seq) == 0
        prev = jnp.where(first, 0.0, carry_ref[...])
        p6 = prev[6:7, :]
        p7 = prev[7:8, :]
        m1 = jnp.where(rows == 0, p7, r1)
        m2 = jnp.where(rows == 0, p6, jnp.where(rows == 1, p7, r2))
        carry_ref[...] = xc[tm - 8:tm, :]
    y = m2 * wconv_ref[0:1, :] + m1 * wconv_ref[1:2, :] + xc * wconv_ref[2:3, :]
    bconv_ref[...] = (cb * y).astype(BF16)

    u = proj(_OFF_U, _OFF_V)
    v = proj(_OFF_V, _OFF_KPE)
    vc = v - jnp.mean(v, axis=-1, keepdims=True)
    vn = vc * lax.rsqrt(jnp.mean(vc * vc, axis=-1, keepdims=True) + EPS) * gv_ref[...] + bv_ref[...]
    vn_ref[...] = vn
    vnb = vn.astype(BF16)
    gw = W_C // N_GROUPS_C
    for r0 in range(0, tm, CHUNK):
        for g in range(N_GROUPS_C):
            mix = _dot(ms_ref[g], vnb[r0:r0 + CHUNK, g * gw:(g + 1) * gw]) + bs_ref[:, g * gw:(g + 1) * gw]
            cmix_ref[r0:r0 + CHUNK, g * gw:(g + 1) * gw] = (
                u[r0:r0 + CHUNK, g * gw:(g + 1) * gw] * mix).astype(BF16)

    sga_ref[...] = jax.nn.sigmoid(_dot(h, wg_ref[:, 0:d_model]))
    sgb_ref[...] = jax.nn.sigmoid(_dot(h, wg_ref[:, d_model:2 * d_model]))
    sgc_ref[...] = jax.nn.sigmoid(_dot(h, wg_ref[:, 2 * d_model:3 * d_model]))


def _inproj(x, lw, tabs, sample, seq_len, conv_fix, tm):
    t, d = x.shape
    nblk = t // tm
    blocks_per_seq = max(seq_len // tm, 1)
    row = lambda w: pl.BlockSpec((tm, w), lambda i: (i, 0))
    if sample:
        tab_spec = lambda w: pl.BlockSpec((tm, w), lambda i: (0, 0))
    else:
        tab_spec = lambda w: pl.BlockSpec((tm, w), lambda i: (i % blocks_per_seq, 0))
    n_pe = N_HEADS_A * ROPE_DIM
    in_specs = [
        row(d), _const_spec((1, d)), _const_spec(lw["w_small"].shape), _const_spec(lw["w_gates"].shape),
        _const_spec((1, Q_RANK)), _const_spec((1, KV_RANK)), _const_spec(lw["w_uq"].shape),
        _const_spec(lw["w_uk_bd"].shape),
        tab_spec(ROPE_DIM), tab_spec(ROPE_DIM), tab_spec(n_pe), tab_spec(n_pe),
        _const_spec((CONV_W, W_B)), _const_spec((1, W_C)), _const_spec((1, W_C)),
        _const_spec((N_GROUPS_C, CHUNK, CHUNK)), _const_spec((CHUNK, W_C)),
    ]
    args = [x, lw["g_attn"], lw["w_small"], lw["w_gates"], lw["g_q"], lw["g_kv"], lw["w_uq"], lw["w_uk_bd"],
            tabs["cosk"], tabs["sink"], tabs["cosq"], tabs["sinq"], lw["w_conv"], lw["g_v"], lw["b_v"],
            lw["ms_s"] if sample else lw["ms_p"], lw["bs_s"] if sample else lw["bs_p"]]
    scratch = []
    if sample:
        in_specs += [row(W_B), row(W_B)]
        args += [conv_fix[0], conv_fix[1]]
    else:
        scratch = [pltpu.VMEM((8, W_B), F32)]
    n_qcols = N_HEADS_A * Q_BLOCK
    if sample:
        q_shapes = [jax.ShapeDtypeStruct((N_HEADS_A, t, KV_RANK), F32),
                    jax.ShapeDtypeStruct((N_HEADS_A, t, ROPE_DIM), F32)]
        q_specs = [pl.BlockSpec((N_HEADS_A, tm, KV_RANK), lambda i: (0, i, 0)),
                   pl.BlockSpec((N_HEADS_A, tm, ROPE_DIM), lambda i: (0, i, 0))]
    else:
        q_shapes = [jax.ShapeDtypeStruct((t // Q_BLOCK, KV_RANK, n_qcols), BF16),
                    jax.ShapeDtypeStruct((t // Q_BLOCK, ROPE_DIM, n_qcols), BF16)]
        q_specs = [pl.BlockSpec((tm // Q_BLOCK, KV_RANK, n_qcols), lambda i: (i, 0, 0)),
                   pl.BlockSpec((tm // Q_BLOCK, ROPE_DIM, n_qcols), lambda i: (i, 0, 0))]
    out_shape = q_shapes + [
        jax.ShapeDtypeStruct((t, KV_RANK), F32),
        jax.ShapeDtypeStruct((t, ROPE_DIM), F32),
        jax.ShapeDtypeStruct((t, KV_RANK), BF16),
        jax.ShapeDtypeStruct((t, ROPE_DIM), BF16),
        jax.ShapeDtypeStruct((t, W_B), BF16),
        jax.ShapeDtypeStruct((t, W_B), F32),
        jax.ShapeDtypeStruct((t, W_C), BF16),
        jax.ShapeDtypeStruct((t, W_C), F32),
        jax.ShapeDtypeStruct((t, d), F32),
        jax.ShapeDtypeStruct((t, d), F32),
        jax.ShapeDtypeStruct((t, d), F32),
    ]
    out_specs = q_specs + [row(KV_RANK), row(ROPE_DIM), row(KV_RANK),
                           row(ROPE_DIM), row(W_B), row(W_B), row(W_C), row(W_C), row(d), row(d), row(d)]
    if not sample:
        out_shape.append(jax.ShapeDtypeStruct((KV_RANK, t), BF16))
        out_specs.append(pl.BlockSpec((KV_RANK, tm), lambda i: (0, i)))
    return pl.pallas_call(
        functools.partial(_inproj_body, sample, tm, blocks_per_seq, d),
        grid=(nblk,),
        in_specs=in_specs,
        out_specs=out_specs,
        out_shape=out_shape,
        scratch_shapes=scratch,
        compiler_params=_cparams(("arbitrary",)),
        name="inproj_sample" if sample else "inproj_prompt",
    )(*args)


ATTN_COL_GROUP = 1024


def _attn_prompt_body(kb, qlt_ref, qpt_ref, kc_ref, kp_ref, kct_ref, o_ref, m_ref, l_ref, acc_ref):
    qi = pl.program_id(1)
    kj = pl.program_id(2)
    cols_all = N_HEADS_A * Q_BLOCK
    last = ((qi + 1) * Q_BLOCK - 1) // kb

    @pl.when(kj == 0)
    def _():
        m_ref[...] = jnp.full((1, cols_all), NEG_BIG, F32)
        l_ref[...] = jnp.zeros((1, cols_all), F32)
        acc_ref[...] = jnp.zeros((KV_RANK, cols_all), F32)

    def step(masked):
        kc = kc_ref[...]
        kp = kp_ref[...]
        kct = kct_ref[...]
        for c0 in range(0, cols_all, ATTN_COL_GROUP):
            cols = slice(c0, c0 + ATTN_COL_GROUP)
            s = _dot(kc, qlt_ref[:, cols]) + _dot(kp, qpt_ref[:, cols])
            if masked:
                kpos = kj * kb + lax.broadcasted_iota(jnp.int32, (kb, 1), 0)
                qpos = qi * Q_BLOCK + lax.broadcasted_iota(jnp.int32, (1, ATTN_COL_GROUP), 1) % Q_BLOCK
                s = jnp.where(kpos <= qpos, s, NEG_BIG)
            m_prev = m_ref[:, cols]
            m_next = jnp.maximum(m_prev, jnp.max(s, axis=0, keepdims=True))
            p = jnp.exp(s - m_next)
            alpha = jnp.exp(m_prev - m_next)
            l_ref[:, cols] = l_ref[:, cols] * alpha + jnp.sum(p, axis=0, keepdims=True)
            acc_ref[:, cols] = acc_ref[:, cols] * alpha + _dot(kct, p.astype(BF16))
            m_ref[:, cols] = m_next

    @pl.when(kj < last)
    def _():
        step(False)

    @pl.when(kj == last)
    def _():
        step(True)
        o_t = acc_ref[...] / l_ref[...]
        for hd in range(N_HEADS_A):
            o_ref[hd] = o_t[:, hd * Q_BLOCK:(hd + 1) * Q_BLOCK].T.astype(BF16)


def _attn_prompt(qlt, qpt, kcb, kpb, kct, batch, seq, kb):
    nq = seq // Q_BLOCK
    nk = seq // kb
    t = batch * seq
    cols_all = N_HEADS_A * Q_BLOCK

    def kblk(b, qi, kj):
        return b * nk + jnp.minimum(kj, ((qi + 1) * Q_BLOCK - 1) // kb)

    qmap = lambda b, qi, kj: (b * nq + qi, 0, 0)
    return pl.pallas_call(
        functools.partial(_attn_prompt_body, kb),
        grid=(batch, nq, nk),
        in_specs=[
            pl.BlockSpec((None, KV_RANK, cols_all), qmap),
            pl.BlockSpec((None, ROPE_DIM, cols_all), qmap),
            pl.BlockSpec((kb, KV_RANK), lambda b, qi, kj: (kblk(b, qi, kj), 0)),
            pl.BlockSpec((kb, ROPE_DIM), lambda b, qi, kj: (kblk(b, qi, kj), 0)),
            pl.BlockSpec((KV_RANK, kb), lambda b, qi, kj: (0, kblk(b, qi, kj))),
        ],
        out_specs=pl.BlockSpec((N_HEADS_A, Q_BLOCK, KV_RANK), lambda b, qi, kj: (0, b * nq + qi, 0)),
        out_shape=jax.ShapeDtypeStruct((N_HEADS_A, t, KV_RANK), BF16),
        scratch_shapes=[pltpu.VMEM((1, cols_all), F32), pltpu.VMEM((1, cols_all), F32),
                        pltpu.VMEM((KV_RANK, cols_all), F32)],
        compiler_params=_cparams(("arbitrary", "arbitrary", "arbitrary")),
        name="attn_prompt",
    )(qlt, qpt, kcb, kpb, kct)


SAMPLE_KEY_CHUNK = 2048


def _attn_sample_dma_body(layer, n_pages, dec_seq, page, pt_ref, qlat_ref, qpe_ref, knc_ref, knp_ref,
                          ckv_hbm, kpet_hbm, o_ref, kbuf0, kbuf1, pbuf0, pbuf1, sem):
    g = pl.program_id(0)
    ng = pl.num_programs(0)
    rows = N_HEADS_A * dec_seq
    kbufs, pbufs = (kbuf0, kbuf1), (pbuf0, pbuf1)

    def page_copies(seq, slot):
        copies = []
        for p in range(n_pages):
            pg = pt_ref[seq, p]
            copies.append(pltpu.make_async_copy(
                ckv_hbm.at[layer, pg], kbufs[slot].at[pl.ds(p * page, page), :], sem.at[0, slot]))
            copies.append(pltpu.make_async_copy(
                kpet_hbm.at[layer, pg], pbufs[slot].at[:, pl.ds(p * page, page)], sem.at[1, slot]))
        return copies

    def start_all(seq, slot):
        for c in page_copies(seq, slot):
            c.start()

    def wait_all(seq, slot):
        for c in page_copies(seq, slot):
            c.wait()

    def attend(slot):
        r0 = slot * dec_seq
        q = jnp.concatenate([qlat_ref[hd, r0:r0 + dec_seq, :] for hd in range(N_HEADS_A)], axis=0).astype(BF16)
        qp = jnp.concatenate([qpe_ref[hd, r0:r0 + dec_seq, :] for hd in range(N_HEADS_A)], axis=0).astype(BF16)
        chunks = range(0, n_pages * page, SAMPLE_KEY_CHUNK)
        ks = [kbufs[slot][c0:c0 + SAMPLE_KEY_CHUNK, :].astype(BF16) for c0 in chunks]
        ss = [_dot_nt(q, k) + _dot(qp, pbufs[slot][:, c0:c0 + SAMPLE_KEY_CHUNK].astype(BF16))
              for k, c0 in zip(ks, chunks)]
        kn = knc_ref[slot]
        qpos = lax.broadcasted_iota(jnp.int32, (rows, 1), 0) % dec_seq
        kpos = lax.broadcasted_iota(jnp.int32, (1, page), 1)
        sn = jnp.where(kpos <= qpos, _dot_nt(q, kn) + _dot(qp, knp_ref[slot]), NEG_BIG)
        m = jnp.max(sn, axis=1, keepdims=True)
        for s in ss:
            m = jnp.maximum(m, jnp.max(s, axis=1, keepdims=True))
        pn = jnp.exp(sn - m)
        l = jnp.sum(pn, axis=1, keepdims=True)
        acc = _dot(pn.astype(BF16), kn)
        for s, k in zip(ss, ks):
            p = jnp.exp(s - m)
            l = l + jnp.sum(p, axis=1, keepdims=True)
            acc = acc + _dot(p.astype(BF16), k)
        o = acc / l
        for hd in range(N_HEADS_A):
            o_ref[hd, r0:r0 + dec_seq, :] = o[hd * dec_seq:(hd + 1) * dec_seq, :]

    @pl.when(g == 0)
    def _():
        start_all(0, 0)

    start_all(2 * g + 1, 1)
    wait_all(2 * g, 0)
    attend(0)

    @pl.when(g + 1 < ng)
    def _():
        start_all(2 * g + 2, 0)

    wait_all(2 * g + 1, 1)
    attend(1)


def _attn_sample_dma(layer, qlat, qpe, cache_ckv, cache_kpe_t, page_table, knew_c, knew_pt, dec_seq):
    dec_b, n_pages = page_table.shape
    page = cache_ckv.shape[2]
    t = dec_b * dec_seq
    assert dec_b % 2 == 0 and (n_pages * page) % SAMPLE_KEY_CHUNK == 0
    pair = lambda w: pl.BlockSpec((N_HEADS_A, 2 * dec_seq, w), lambda g, pt: (0, g, 0))
    grid_spec = pltpu.PrefetchScalarGridSpec(
        num_scalar_prefetch=1,
        grid=(dec_b // 2,),
        in_specs=[pair(KV_RANK), pair(ROPE_DIM),
                  pl.BlockSpec((2, page, KV_RANK), lambda g, pt: (g, 0, 0)),
                  pl.BlockSpec((2, ROPE_DIM, page), lambda g, pt: (g, 0, 0)),
                  pl.BlockSpec(memory_space=pl.ANY), pl.BlockSpec(memory_space=pl.ANY)],
        out_specs=pair(KV_RANK),
        scratch_shapes=[pltpu.VMEM((n_pages * page, KV_RANK), F32), pltpu.VMEM((n_pages * page, KV_RANK), F32),
                        pltpu.VMEM((ROPE_DIM, n_pages * page), F32), pltpu.VMEM((ROPE_DIM, n_pages * page), F32),
                        pltpu.SemaphoreType.DMA((2, 2))],
    )
    return pl.pallas_call(
        functools.partial(_attn_sample_dma_body, layer, n_pages, dec_seq, page),
        grid_spec=grid_spec,
        out_shape=jax.ShapeDtypeStruct((N_HEADS_A, t, KV_RANK), F32),
        compiler_params=_cparams(("arbitrary",)),
        name="attn_sample",
    )(page_table, qlat, qpe, knew_c, knew_pt, cache_ckv, cache_kpe_t)


def _merge_body(x_ref, olat_ref, wuvbd_ref, bconv_ref, cmix_ref, sga_ref, sgb_ref, sgc_ref,
                wbra_ref, wbrb_ref, wbrc_ref, wout_ref, gffn_ref, wpq_ref, skeys_ref,
                x1_ref, h2t_ref, st_ref):
    ocat = jnp.concatenate([olat_ref[hd].astype(BF16) for hd in range(N_HEADS_A)], axis=1)
    a = _dot(ocat, wuvbd_ref[...]).astype(BF16)
    merged = (sga_ref[...] * _dot(a, wbra_ref[...])
              + sgb_ref[...] * _dot(bconv_ref[...], wbrb_ref[...])
              + sgc_ref[...] * _dot(cmix_ref[...], wbrc_ref[...]))
    x1 = x_ref[...] + _dot(merged.astype(BF16), wout_ref[...])
    x1_ref[...] = x1
    h2 = _rms_rows(x1, gffn_ref[...])
    h2t_ref[...] = h2.T.astype(BF16)
    q = _dot(h2.astype(BF16), wpq_ref[...]).astype(BF16)
    half = q.shape[1] // (2 * N_HEADS_P)
    for hc in range(2 * N_HEADS_P):
        st_ref[hc * N_KEYS:(hc + 1) * N_KEYS, :] = _dot_nt(skeys_ref[hc], q[:, hc * half:(hc + 1) * half])


def _merge(x, olat, bconv, cmix, sga, sgb, sgc, lw, tm):
    t, d = x.shape
    row = lambda w: pl.BlockSpec((tm, w), lambda i: (i, 0))
    col = lambda r: pl.BlockSpec((r, tm), lambda i: (0, i))
    n_s = 2 * N_HEADS_P * N_KEYS
    return pl.pallas_call(
        _merge_body,
        grid=(t // tm,),
        in_specs=[row(d), pl.BlockSpec((N_HEADS_A, tm, KV_RANK), lambda i: (0, i, 0)),
                  _const_spec(lw["w_uv_bd"].shape), row(W_B), row(W_C), row(d), row(d), row(d),
                  _const_spec(lw["w_br_a"].shape), _const_spec(lw["w_br_b"].shape),
                  _const_spec(lw["w_br_c"].shape), _const_spec(lw["w_out"].shape), _const_spec((1, d)),
                  _const_spec(lw["w_pq"].shape), _const_spec(lw["skeys"].shape)],
        out_specs=[row(d), col(d), col(n_s)],
        out_shape=[jax.ShapeDtypeStruct((t, d), F32), jax.ShapeDtypeStruct((d, t), BF16),
                   jax.ShapeDtypeStruct((n_s, t), F32)],
        compiler_params=_cparams(("arbitrary",)),
        name="merge",
    )(x, olat, lw["w_uv_bd"], bconv, cmix, sga, sgb, sgc, lw["w_br_a"], lw["w_br_b"], lw["w_br_c"],
      lw["w_out"], lw["g_ffn"], lw["w_pq"], lw["skeys"])


def _top_values(cur, n, out_ref, base):
    for k in range(n):
        mx = jnp.max(cur, axis=0, keepdims=True)
        out_ref[base + k:base + k + 1, :] = mx
        if k + 1 < n:
            cur = jnp.where(cur == mx, -jnp.inf, cur)


def _peer_select_body(st_ref, a0_ref, b1_ref, cthr_ref, top_ref, cand_ref):
    tn = st_ref.shape[1]
    n_half = N_HEADS_P * N_KEYS
    for hc in range(2 * N_HEADS_P):
        _top_values(st_ref[hc * N_KEYS:(hc + 1) * N_KEYS, :], TOPK_P, top_ref, hc * TOPK_P)
    rank = lax.broadcasted_iota(jnp.int32, (TOPK_P, 1), 0)
    for hd in range(N_HEADS_P):
        s0 = top_ref[hd * TOPK_P:(hd + 1) * TOPK_P, :]
        s1 = top_ref[(N_HEADS_P + hd) * TOPK_P:(N_HEADS_P + hd + 1) * TOPK_P, :]
        half = TOPK_P // 2
        s0_lo = s0[0:half, :]
        part0 = s0 + s1[0:1, :]
        mid = [jnp.where(rank[0:half] < TOPK_P // (b + 1), s0_lo + s1[b:b + 1, :], -jnp.inf)
               for b in range(1, half)]
        tail = s0[0:1, :] + s1[half:TOPK_P, :]
        cands = jnp.concatenate([part0] + mid + [tail], axis=0)
        _top_values(cands, TOPK_P, cand_ref, 0)
        thr = cand_ref[TOPK_P - 1:TOPK_P, :]
        mx = cand_ref[0:1, :]
        z = jnp.sum(jnp.where(cands >= thr, jnp.exp(cands - mx), 0.0), axis=0, keepdims=True)
        tau_lo = jnp.full((half, tn), jnp.inf, F32)
        for b in range(1, half):
            tau_lo = jnp.minimum(tau_lo, jnp.where(mid[b - 1] >= thr, s1[b:b + 1, :], jnp.inf))
        tau_0 = jnp.min(jnp.where(tail >= thr, s1[half:TOPK_P, :], jnp.inf), axis=0, keepdims=True)
        tau_lo = jnp.minimum(tau_lo, jnp.where(rank[0:half] == 0, tau_0, jnp.inf))
        tau = jnp.minimum(jnp.where(part0 >= thr, s1[0:1, :], jnp.inf),
                          jnp.concatenate([tau_lo, jnp.full((half, tn), jnp.inf, F32)], axis=0))
        s0_all = st_ref[hd * N_KEYS:(hd + 1) * N_KEYS, :]
        s1_all = st_ref[n_half + hd * N_KEYS:n_half + (hd + 1) * N_KEYS, :]

        def factor1(x):
            return jnp.exp(x - s1[0:1, :]) * (0.5 / z)

        beta = factor1(tau)
        cthr = jnp.full((N_KEYS, tn), jnp.inf, F32)
        for a in reversed(range(TOPK_P)):
            cthr = jnp.where(s0_all >= s0[a:a + 1, :], beta[a:a + 1, :], cthr)
        cthr_ref[hd * N_KEYS:(hd + 1) * N_KEYS, :] = cthr
        a0_ref[hd * N_KEYS:(hd + 1) * N_KEYS, :] = jnp.exp(s0_all - s0[0:1, :])
        b1_ref[hd * N_KEYS:(hd + 1) * N_KEYS, :] = factor1(s1_all)


def _peer_select(st, tn):
    n_s, t = st.shape
    n_h = N_HEADS_P * N_KEYS
    col = lambda r: pl.BlockSpec((r, tn), lambda i: (0, i))
    return pl.pallas_call(
        _peer_select_body,
        grid=(t // tn,),
        in_specs=[col(n_s)],
        out_specs=[col(n_h), col(n_h), col(n_h)],
        out_shape=[jax.ShapeDtypeStruct((n_h, t), F32)] * 3,
        scratch_shapes=[pltpu.VMEM((2 * N_HEADS_P * TOPK_P, tn), F32), pltpu.VMEM((TOPK_P, tn), F32)],
        compiler_params=_cparams(("arbitrary",)),
        name="peer_select",
    )(st)


def _peer_gate_block(item, at, a0_ref, b1_ref, cthr_ref, w_ref, keys_per_block, lane_chunk):
    tn = at.shape[1]
    for c0 in range(0, tn, lane_chunk):
        lanes = slice(c0, c0 + lane_chunk)
        for ii in range(keys_per_block):
            gate = None
            for hd in range(N_HEADS_P):
                b1 = b1_ref[hd * N_KEYS:(hd + 1) * N_KEYS, lanes]
                cthr = cthr_ref[hd, item, ii:ii + 1, lanes]
                a0 = a0_ref[hd, item, ii:ii + 1, lanes]
                term = a0 * jnp.where(b1 >= cthr, b1, 0.0)
                gate = term if gate is None else gate + term
            xa = at[ii * N_KEYS:(ii + 1) * N_KEYS, lanes]
            act2 = xa * (1.0 + lax.erf(xa * (2.0 ** -0.5)))
            w_ref[ii * N_KEYS:(ii + 1) * N_KEYS, lanes] = (gate * act2).astype(BF16)


def _peer_dense_body(eb, lane_chunk, h2t_ref, u_ref, vta_ref, vtb_ref, vtc_ref, a0_ref, b1_ref,
                     cthr_ref, out_ref, w0_ref, w1_ref):
    k = pl.program_id(1)
    nk = pl.num_programs(1)
    keys_per_block = eb // N_KEYS
    gate_args = (a0_ref, b1_ref, cthr_ref)

    @pl.when(k == 0)
    def _():
        out_ref[...] = jnp.zeros(out_ref.shape, F32)
        w1_ref[...] = jnp.zeros(w1_ref.shape, BF16)

    h2t = h2t_ref[...]
    at0 = _dot(u_ref[0:eb, :], h2t)
    out_ref[...] += _dot(vta_ref[...], w1_ref[...])
    _peer_gate_block(0, at0, *gate_args, w0_ref, keys_per_block, lane_chunk)
    at1 = _dot(u_ref[eb:2 * eb, :], h2t)
    out_ref[...] += _dot(vtb_ref[...], w0_ref[...])
    _peer_gate_block(1, at1, *gate_args, w1_ref, keys_per_block, lane_chunk)

    @pl.when(k == nk - 1)
    def _():
        out_ref[...] += _dot(vtc_ref[...], w1_ref[...])


def _peer_dense(h2t, u_bf, vt_bf, a0, b1, cthr, tn, eb):
    d, t = h2t.shape
    n_e = u_bf.shape[0]
    n_blocks = n_e // eb
    assert eb == 8 * N_KEYS
    assert n_blocks % 2 == 0
    col = lambda r: pl.BlockSpec((r, tn), lambda i, k: (0, i))
    n_h = N_HEADS_P * N_KEYS
    keys_per_block = eb // N_KEYS
    by_block = lambda a: a.reshape(N_HEADS_P, n_blocks, keys_per_block, t)
    blk_rows = pl.BlockSpec((N_HEADS_P, 2, keys_per_block, tn), lambda i, k: (0, k, 0, i))
    return pl.pallas_call(
        functools.partial(_peer_dense_body, eb, 128),
        grid=(t // tn, n_blocks // 2),
        in_specs=[col(d), pl.BlockSpec((2 * eb, d), lambda i, k: (k, 0)),
                  pl.BlockSpec((d, eb), lambda i, k: (0, jnp.maximum(2 * k - 1, 0))),
                  pl.BlockSpec((d, eb), lambda i, k: (0, 2 * k)),
                  pl.BlockSpec((d, eb), lambda i, k: (0, n_blocks - 1)),
                  blk_rows, col(n_h), blk_rows],
        out_specs=col(d),
        out_shape=jax.ShapeDtypeStruct((d, t), F32),
        scratch_shapes=[pltpu.VMEM((eb, tn), BF16), pltpu.VMEM((eb, tn), BF16)],
        compiler_params=_cparams(("arbitrary", "arbitrary")),
        name="peer_dense",
    )(h2t, u_bf, vt_bf, vt_bf, vt_bf, by_block(a0), b1, by_block(cthr))


def _table_prep_body(u_ref, v_ref, ub_ref, vt_ref):
    ub_ref[...] = u_ref[...].astype(BF16)
    vt_ref[...] = v_ref[...].T.astype(BF16)


def _table_prep(layer, u_tab, v_tab, rows):
    _, n_e, d = u_tab.shape
    src = pl.BlockSpec((None, rows, d), lambda e: (layer, e, 0))
    return pl.pallas_call(
        _table_prep_body,
        grid=(n_e // rows,),
        in_specs=[src, src],
        out_specs=[pl.BlockSpec((rows, d), lambda e: (e, 0)), pl.BlockSpec((d, rows), lambda e: (0, e))],
        out_shape=[jax.ShapeDtypeStruct((n_e, d), BF16), jax.ShapeDtypeStruct((d, n_e), BF16)],
        compiler_params=_cparams(("arbitrary",)),
        name="table_prep",
    )(u_tab, v_tab)


def _ple_body(last, x1_ref, pt_ref, gple_ref, wpg_ref, p_ref, wple_ref, *rest):
    if last:
        gfin_ref, x3_ref, y_ref = rest
    else:
        (x3_ref,) = rest
    x2 = x1_ref[...] + pt_ref[...].T
    hp = _rms_rows(x2, gple_ref[...]).astype(BF16)
    gate = jax.nn.sigmoid(_dot(hp, wpg_ref[...]))
    x3 = x2 + gate * _dot(p_ref[...].astype(BF16), wple_ref[...])
    x3_ref[...] = x3
    if last:
        y_ref[...] = _rms_rows(x3, gfin_ref[...])


def _ple(x1, peer_t, col_off, p_i, lw, g_final, last, tm):
    t, d = x1.shape
    row = lambda w: pl.BlockSpec((tm, w), lambda i: (i, 0))
    in_specs = [row(d), pl.BlockSpec((d, tm), lambda i: (0, i + col_off)), _const_spec((1, d)),
                _const_spec(lw["w_pg"].shape), row(p_i.shape[1]), _const_spec(lw["w_ple"].shape)]
    args = [x1, peer_t, lw["g_ple"], lw["w_pg"], p_i, lw["w_ple"]]
    out_specs = [row(d)]
    out_shape = [jax.ShapeDtypeStruct((t, d), F32)]
    if last:
        in_specs.append(_const_spec((1, d)))
        args.append(g_final)
        out_specs.append(row(d))
        out_shape.append(jax.ShapeDtypeStruct((t, d), F32))
    return pl.pallas_call(
        functools.partial(_ple_body, last),
        grid=(t // tm,),
        in_specs=in_specs,
        out_specs=out_specs,
        out_shape=out_shape,
        compiler_params=_cparams(("arbitrary",)),
        name="ple_last" if last else "ple",
    )(*args)


def _rope_tables(pos):
    inv = 1.0 / (ROPE_THETA ** (jnp.arange(0, ROPE_DIM, 2, dtype=F32) / ROPE_DIM))
    ang = pos.astype(F32)[:, None] * inv[None, :]
    cos, sin = jnp.cos(ang), jnp.sin(ang)
    cosk = jnp.concatenate([cos, cos], axis=1)
    sink = jnp.concatenate([-sin, sin], axis=1)
    return {"cosk": cosk, "sink": sink, "cosq": jnp.tile(cosk, (1, N_HEADS_A)),
            "sinq": jnp.tile(sink, (1, N_HEADS_A))}


def _swap_halves(w):
    half = w.shape[-1] // 2
    return jnp.concatenate([w[..., half:], w[..., :half]], axis=-1)


def _block_diag(blocks):
    n, r, c = blocks.shape
    eye = jnp.eye(n, dtype=blocks.dtype)
    return (eye[:, None, :, None] * blocks[:, :, None, :]).reshape(n * r, n * c)


def _layer_weights(i, d, dec_seq, g_attn, w_in, g_q, g_kv, w_uq, w_uk, w_uv, w_conv, g_v, b_v, w_s, b_s,
                   w_br_a, w_br_b, w_br_c, w_out, g_ffn, w_pq, sub_keys, u_tab, v_tab, g_ple, w_pg, w_ple):
    bf = lambda a: a.astype(BF16)
    wi = w_in[i]
    o_kpe = Q_RANK + KV_RANK
    o_cb = o_kpe + ROPE_DIM
    o_g = o_cb + 3 * W_B + 2 * W_C
    w_kpe = wi[:, o_kpe:o_cb]
    w_small = jnp.concatenate([wi[:, :o_kpe], wi[:, o_cb:o_g], w_kpe, _swap_halves(w_kpe)], axis=1)
    uq = w_uq[i]
    uq_pe = uq[:, :, NOPE_DIM:]
    w_uq_p = jnp.concatenate([uq[:, :, :NOPE_DIM].reshape(Q_RANK, -1), uq_pe.reshape(Q_RANK, -1),
                              _swap_halves(uq_pe).reshape(Q_RANK, -1)], axis=1)
    ws = jnp.tril(w_s[i])
    ws_small = jnp.tril(w_s[i][:, :dec_seq, :dec_seq])
    eye = jnp.eye(CHUNK // dec_seq, dtype=F32)
    ms_s = jnp.einsum("ab,gts->gatbs", eye, ws_small).reshape(N_GROUPS_C, CHUNK, CHUNK)
    gw = W_C // N_GROUPS_C
    bs_p = jnp.repeat(b_s[i].T, gw, axis=1)
    bs_s = jnp.tile(bs_p[:dec_seq], (CHUNK // dec_seq, 1))
    return {
        "g_attn": g_attn[i][None], "w_small": bf(w_small), "w_gates": bf(wi[:, o_g:]),
        "g_q": g_q[i][None], "g_kv": g_kv[i][None], "w_uq": bf(w_uq_p),
        "w_uk_bd": bf(_block_diag(jnp.transpose(w_uk[i], (1, 2, 0)))),
        "w_uv_bd": bf(_block_diag(jnp.transpose(w_uv[i], (1, 0, 2)))),
        "w_conv": w_conv[i], "g_v": g_v[i][None], "b_v": b_v[i][None],
        "ms_p": bf(ws), "ms_s": bf(ms_s), "bs_p": bs_p, "bs_s": bs_s,
        "w_br_a": bf(w_br_a[i]), "w_br_b": bf(w_br_b[i]), "w_br_c": bf(w_br_c[i]), "w_out": bf(w_out[i]),
        "g_ffn": g_ffn[i][None], "w_pq": bf(jnp.transpose(w_pq[i], (0, 2, 1, 3)).reshape(d, -1)),
        "skeys": bf(jnp.transpose(sub_keys[i], (1, 0, 2, 3)).reshape(2 * N_HEADS_P, N_KEYS, -1)),
        "g_ple": g_ple[i][None], "w_pg": bf(w_pg[i]), "w_ple": bf(w_ple[i]),
    }


def _pick_tile(n, prefs):
    for p in prefs:
        if n % p == 0:
            return p
    raise ValueError(f"no tile for {n}")


def kernel(x_prompt, x_sample, p_prompt, p_sample, cache_ckv, cache_kpe, state_conv, page_table,
           g_attn, w_in, g_q, g_kv, w_uq, w_uk, w_uv, w_conv, g_v, b_v, w_s, b_s,
           w_br_a, w_br_b, w_br_c, w_out, g_ffn, w_pq, sub_keys, u_tab, v_tab,
           g_ple, w_pg, w_ple, g_final):
    batch, seq, d = x_prompt.shape
    dec_b, dec_seq, _ = x_sample.shape
    depth = w_in.shape[0]
    n_pages = page_table.shape[1]
    page = cache_ckv.shape[2]
    past_len = n_pages * page
    tp, ts = batch * seq, dec_b * dec_seq
    assert seq % CHUNK == 0 and CHUNK % dec_seq == 0 and ts % CHUNK == 0 and dec_seq == 8
    assert page == CHUNK

    tm_p = _pick_tile(seq, (256, 128))
    tm_s = _pick_tile(ts, (256, 128))
    kb = _pick_tile(seq, (512, 256, 128))
    tn_p = _pick_tile(tp, (512, 256, 128))
    tn_s = _pick_tile(ts, (512, 256, 128))
    eb = 1024

    tabs_p = _rope_tables(jnp.arange(seq, dtype=jnp.int32))
    tabs_s = _rope_tables(past_len + (jnp.arange(tm_s, dtype=jnp.int32) % dec_seq))
    g_fin = g_final[None]
    cache_kpe_t = jnp.swapaxes(cache_kpe, 2, 3)

    xp = x_prompt.reshape(tp, d)
    xs = x_sample.reshape(ts, d)
    outs = {k: [] for k in ("ckv_p", "kpe_p", "ckv_s", "kpe_s", "conv_p", "conv_s", "v_s")}
    y_p = y_s = None
    for i in range(depth):
        lw = _layer_weights(i, d, dec_seq, g_attn, w_in, g_q, g_kv, w_uq, w_uk, w_uv, w_conv, g_v, b_v,
                            w_s, b_s, w_br_a, w_br_b, w_br_c, w_out, g_ffn, w_pq, sub_keys, u_tab, v_tab,
                            g_ple, w_pg, w_ple)
        last = i == depth - 1

        (qlt_p, qpt_p, ckv_p, kpe_p, kcb_p, kpb_p, bconv_p, xc_p, cmix_p, _vn_p, sga_p, sgb_p, sgc_p,
         kct_p) = _inproj(xp, lw, tabs_p, False, seq, None, tm_p)
        olat_p = _attn_prompt(qlt_p, qpt_p, kcb_p, kpb_p, kct_p, batch, seq, kb)
        x1_p, h2t_p, st_p = _merge(xp, olat_p, bconv_p, cmix_p, sga_p, sgb_p, sgc_p, lw, tm_p)

        st_c = state_conv[i]
        zero_row = jnp.zeros((dec_b, 1, W_B), F32)
        f1 = jnp.concatenate([st_c[:, 1:2], jnp.zeros((dec_b, dec_seq - 1, W_B), F32)], axis=1).reshape(ts, W_B)
        f2 = jnp.concatenate([st_c[:, 0:1], st_c[:, 1:2], jnp.zeros((dec_b, dec_seq - 2, W_B), F32)],
                             axis=1).reshape(ts, W_B)
        del zero_row
        (qlat_s, qpe_s, ckv_s, kpe_s, kcb_s, kpb_s, bconv_s, xc_s, cmix_s, vn_s, sga_s, sgb_s, sgc_s) = _inproj(
            xs, lw, tabs_s, True, dec_seq, (f1, f2), tm_s)
        knew_c = jnp.pad(kcb_s.reshape(dec_b, dec_seq, KV_RANK), ((0, 0), (0, page - dec_seq), (0, 0)))
        knew_p = jnp.pad(jnp.swapaxes(kpb_s.reshape(dec_b, dec_seq, ROPE_DIM), 1, 2),
                         ((0, 0), (0, 0), (0, page - dec_seq)))
        olat_s = _attn_sample_dma(i, qlat_s, qpe_s, cache_ckv, cache_kpe_t, page_table, knew_c, knew_p, dec_seq)
        x1_s, h2t_s, st_s = _merge(xs, olat_s, bconv_s, cmix_s, sga_s, sgb_s, sgc_s, lw, tm_s)

        u_bf, vt_bf = _table_prep(i, u_tab, v_tab, eb)
        peer_p = _peer_dense(h2t_p, u_bf, vt_bf, *_peer_select(st_p, tn_p), tn_p, eb)
        peer_s = _peer_dense(h2t_s, u_bf, vt_bf, *_peer_select(st_s, tn_s), tn_s, eb)

        res_p = _ple(x1_p, peer_p, 0, p_prompt[i].reshape(tp, -1), lw, g_fin, last, tm_p)
        res_s = _ple(x1_s, peer_s, 0, p_sample[i].reshape(ts, -1), lw, g_fin, last, tm_s)
        xp, xs = res_p[0], res_s[0]
        if last:
            y_p, y_s = res_p[1], res_s[1]

        outs["ckv_p"].append(ckv_p.reshape(batch, seq, KV_RANK))
        outs["kpe_p"].append(kpe_p.reshape(batch, seq, ROPE_DIM))
        outs["ckv_s"].append(ckv_s.reshape(dec_b, dec_seq, KV_RANK))
        outs["kpe_s"].append(kpe_s.reshape(dec_b, dec_seq, ROPE_DIM))
        outs["conv_p"].append(xc_p.reshape(batch, seq, W_B)[:, seq - (CONV_W - 1):])
        outs["conv_s"].append(xc_s.reshape(dec_b, dec_seq, W_B)[:, dec_seq - (CONV_W - 1):])
        outs["v_s"].append(vn_s.reshape(dec_b, dec_seq, W_C))

    return (y_p.reshape(batch, seq, d), y_s.reshape(dec_b, dec_seq, d),
            jnp.stack(outs["ckv_p"]), jnp.stack(outs["kpe_p"]), jnp.stack(outs["ckv_s"]),
            jnp.stack(outs["kpe_s"]), jnp.stack(outs["conv_p"]), jnp.stack(outs["conv_s"]),
            jnp.stack(outs["v_s"]))
```

```python
import functools

import jax
import jax.numpy as jnp
from jax import lax
from jax.experimental import pallas as pl
from jax.experimental.pallas import tpu as pltpu

F32 = jnp.float32
BF16 = jnp.bfloat16

N_HEADS_A = 8
NOPE_DIM = 64
ROPE_DIM = 32
V_DIM = 64
Q_RANK = 768
KV_RANK = 256
ROPE_THETA = 10000.0
Q_BLOCK = 128
MLA_SCALE = (NOPE_DIM + ROPE_DIM) ** -0.5
W_B = 512
CONV_W = 3
W_C = 512
N_GROUPS_C = 4
CHUNK = 128
N_KEYS = 128
N_HEADS_P = 8
TOPK_P = 16
EPS = 1e-6

NEG_BIG = -1e30
VMEM_LIMIT_BYTES = 56 * 1024 * 1024

_OFF_CQ = 0
_OFF_CKV = _OFF_CQ + Q_RANK
_OFF_CB = _OFF_CKV + KV_RANK
_OFF_CC = _OFF_CB + W_B
_OFF_CH = _OFF_CC + W_B
_OFF_U = _OFF_CH + W_B
_OFF_V = _OFF_U + W_C
_OFF_KPE = _OFF_V + W_C
_OFF_KPE_SW = _OFF_KPE + ROPE_DIM
_N_SMALL = _OFF_KPE_SW + ROPE_DIM


def _cparams(semantics):
    return pltpu.CompilerParams(dimension_semantics=semantics, vmem_limit_bytes=VMEM_LIMIT_BYTES)


def _rms_rows(x, g):
    return x * lax.rsqrt(jnp.mean(x * x, axis=-1, keepdims=True) + EPS) * g


def _dot(a, b):
    return jnp.dot(a, b, preferred_element_type=F32)


def _dot_nt(a, b):
    return lax.dot_general(a, b, (((1,), (1,)), ((), ())), preferred_element_type=F32)


def _const_spec(shape):
    nd = len(shape)
    return pl.BlockSpec(shape, lambda *_: (0,) * nd)


def _inproj_body(sample, tm, blocks_per_seq, d_model, *refs):
    if sample:
        (x_ref, gattn_ref, wsm_ref, wg_ref, gq_ref, gkv_ref, wuq_ref, wukbd_ref, cosk_ref, sink_ref,
         cosq_ref, sinq_ref, wconv_ref, gv_ref, bv_ref, ms_ref, bs_ref, f1_ref, f2_ref,
         qlat_ref, qpe_ref, ckv_ref, kpe_ref, kcb_ref, kpb_ref, bconv_ref, xc_ref, cmix_ref, vn_ref,
         sga_ref, sgb_ref, sgc_ref) = refs
        carry_ref = kct_ref = None
    else:
        (x_ref, gattn_ref, wsm_ref, wg_ref, gq_ref, gkv_ref, wuq_ref, wukbd_ref, cosk_ref, sink_ref,
         cosq_ref, sinq_ref, wconv_ref, gv_ref, bv_ref, ms_ref, bs_ref,
         qlat_ref, qpe_ref, ckv_ref, kpe_ref, kcb_ref, kpb_ref, bconv_ref, xc_ref, cmix_ref, vn_ref,
         sga_ref, sgb_ref, sgc_ref, kct_ref, carry_ref) = refs

    x = x_ref[...]
    h = _rms_rows(x, gattn_ref[...]).astype(BF16)

    def proj(a, b):
        return _dot(h, wsm_ref[:, a:b])

    cq = proj(_OFF_CQ, _OFF_CKV)
    cqn = _rms_rows(cq, gq_ref[...]).astype(BF16)
    qall = _dot(cqn, wuq_ref[...])
    n_nope = N_HEADS_A * NOPE_DIM
    n_pe = N_HEADS_A * ROPE_DIM
    qpe = (qall[:, n_nope:n_nope + n_pe] * cosq_ref[...]
           + qall[:, n_nope + n_pe:n_nope + 2 * n_pe] * sinq_ref[...]) * MLA_SCALE
    qlat = _dot(qall[:, :n_nope].astype(BF16), wukbd_ref[...]) * MLA_SCALE
    if sample:
        for hd in range(N_HEADS_A):
            qlat_ref[hd] = qlat[:, hd * KV_RANK:(hd + 1) * KV_RANK]
            qpe_ref[hd] = qpe[:, hd * ROPE_DIM:(hd + 1) * ROPE_DIM]
    else:
        for blk in range(tm // Q_BLOCK):
            r0 = blk * Q_BLOCK
            qpe_t = qpe[r0:r0 + Q_BLOCK, :].T
            for hd in range(N_HEADS_A):
                cols = slice(hd * Q_BLOCK, (hd + 1) * Q_BLOCK)
                qlat_ref[blk, :, cols] = qlat[r0:r0 + Q_BLOCK, hd * KV_RANK:(hd + 1) * KV_RANK].T.astype(BF16)
                qpe_ref[blk, :, cols] = qpe_t[hd * ROPE_DIM:(hd + 1) * ROPE_DIM, :].astype(BF16)

    ckv = _rms_rows(proj(_OFF_CKV, _OFF_CB), gkv_ref[...])
    ckv_ref[...] = ckv
    kcb_ref[...] = ckv.astype(BF16)
    if not sample:
        kct_ref[...] = ckv.T.astype(BF16)
    kpe = proj(_OFF_KPE, _OFF_KPE_SW) * cosk_ref[...] + proj(_OFF_KPE_SW, _N_SMALL) * sink_ref[...]
    kpe_ref[...] = kpe
    kpb_ref[...] = kpe.astype(BF16)

    cb = proj(_OFF_CB, _OFF_CC)
    xc = proj(_OFF_CC, _OFF_CH) * proj(_OFF_CH, _OFF_U)
    xc_ref[...] = xc
    rows = lax.broadcasted_iota(jnp.int32, (tm, 1), 0)
    r1 = pltpu.roll(xc, 1, 0)
    r2 = pltpu.roll(xc, 2, 0)
    if sample:
        pos = rows % 8
        m1 = jnp.where(pos == 0, f1_ref[...], r1)
        m2 = jnp.where(pos < 2, f2_ref[...], r2)
    else:
        first = (pl.program_id(0) % blocks_per_seq) == 0
        prev = jnp.where(first, 0.0, carry_ref[...])
        p6 = prev[6:7, :]
        p7 = prev[7:8, :]
        m1 = jnp.where(rows == 0, p7, r1)
        m2 = jnp.where(rows == 0, p6, jnp.where(rows == 1, p7, r2))
        carry_ref[...] = xc[tm - 8:tm, :]
    y = m2 * wconv_ref[0:1, :] + m1 * wconv_ref[1:2, :] + xc * wconv_ref[2:3, :]
    bconv_ref[...] = (cb * y).astype(BF16)

    u = proj(_OFF_U, _OFF_V)
    v = proj(_OFF_V, _OFF_KPE)
    vc = v - jnp.mean(v, axis=-1, keepdims=True)
    vn = vc * lax.rsqrt(jnp.mean(vc * vc, axis=-1, keepdims=True) + EPS) * gv_ref[...] + bv_ref[...]
    vn_ref[...] = vn
    vnb = vn.astype(BF16)
    gw = W_C // N_GROUPS_C
    for r0 in range(0, tm, CHUNK):
        for g in range(N_GROUPS_C):
            mix = _dot(ms_ref[g], vnb[r0:r0 + CHUNK, g * gw:(g + 1) * gw]) + bs_ref[:, g * gw:(g + 1) * gw]
            cmix_ref[r0:r0 + CHUNK, g * gw:(g + 1) * gw] = (
                u[r0:r0 + CHUNK, g * gw:(g + 1) * gw] * mix).astype(BF16)

    sga_ref[...] = jax.nn.sigmoid(_dot(h, wg_ref[:, 0:d_model]))
    sgb_ref[...] = jax.nn.sigmoid(_dot(h, wg_ref[:, d_model:2 * d_model]))
    sgc_ref[...] = jax.nn.sigmoid(_dot(h, wg_ref[:, 2 * d_model:3 * d_model]))


def _inproj(x, lw, tabs, sample, seq_len, conv_fix, tm):
    t, d = x.shape
    nblk = t // tm
    blocks_per_seq = max(seq_len // tm, 1)
    row = lambda w: pl.BlockSpec((tm, w), lambda i: (i, 0))
    if sample:
        tab_spec = lambda w: pl.BlockSpec((tm, w), lambda i: (0, 0))
    else:
        tab_spec = lambda w: pl.BlockSpec((tm, w), lambda i: (i % blocks_per_seq, 0))
    n_pe = N_HEADS_A * ROPE_DIM
    in_specs = [
        row(d), _const_spec((1, d)), _const_spec(lw["w_small"].shape), _const_spec(lw["w_gates"].shape),
        _const_spec((1, Q_RANK)), _const_spec((1, KV_RANK)), _const_spec(lw["w_uq"].shape),
        _const_spec(lw["w_uk_bd"].shape),
        tab_spec(ROPE_DIM), tab_spec(ROPE_DIM), tab_spec(n_pe), tab_spec(n_pe),
        _const_spec((CONV_W, W_B)), _const_spec((1, W_C)), _const_spec((1, W_C)),
        _const_spec((N_GROUPS_C, CHUNK, CHUNK)), _const_spec((CHUNK, W_C)),
    ]
    args = [x, lw["g_attn"], lw["w_small"], lw["w_gates"], lw["g_q"], lw["g_kv"], lw["w_uq"], lw["w_uk_bd"],
            tabs["cosk"], tabs["sink"], tabs["cosq"], tabs["sinq"], lw["w_conv"], lw["g_v"], lw["b_v"],
            lw["ms_s"] if sample else lw["ms_p"], lw["bs_s"] if sample else lw["bs_p"]]
    scratch = []
    if sample:
        in_specs += [row(W_B), row(W_B)]
        args += [conv_fix[0], conv_fix[1]]
    else:
        scratch = [pltpu.VMEM((8, W_B), F32)]
    n_qcols = N_HEADS_A * Q_BLOCK
    if sample:
        q_shapes = [jax.ShapeDtypeStruct((N_HEADS_A, t, KV_RANK), F32),
                    jax.ShapeDtypeStruct((N_HEADS_A, t, ROPE_DIM), F32)]
        q_specs = [pl.BlockSpec((N_HEADS_A, tm, KV_RANK), lambda i: (0, i, 0)),
                   pl.BlockSpec((N_HEADS_A, tm, ROPE_DIM), lambda i: (0, i, 0))]
    else:
        q_shapes = [jax.ShapeDtypeStruct((t // Q_BLOCK, KV_RANK, n_qcols), BF16),
                    jax.ShapeDtypeStruct((t // Q_BLOCK, ROPE_DIM, n_qcols), BF16)]
        q_specs = [pl.BlockSpec((tm // Q_BLOCK, KV_RANK, n_qcols), lambda i: (i, 0, 0)),
                   pl.BlockSpec((tm // Q_BLOCK, ROPE_DIM, n_qcols), lambda i: (i, 0, 0))]
    out_shape = q_shapes + [
        jax.ShapeDtypeStruct((t, KV_RANK), F32),
        jax.ShapeDtypeStruct((t, ROPE_DIM), F32),
        jax.ShapeDtypeStruct((t, KV_RANK), BF16),
        jax.ShapeDtypeStruct((t, ROPE_DIM), BF16),
        jax.ShapeDtypeStruct((t, W_B), BF16),
        jax.ShapeDtypeStruct((t, W_B), F32),
        jax.ShapeDtypeStruct((t, W_C), BF16),
        jax.ShapeDtypeStruct((t, W_C), F32),
        jax.ShapeDtypeStruct((t, d), F32),
        jax.ShapeDtypeStruct((t, d), F32),
        jax.ShapeDtypeStruct((t, d), F32),
    ]
    out_specs = q_specs + [row(KV_RANK), row(ROPE_DIM), row(KV_RANK),
                           row(ROPE_DIM), row(W_B), row(W_B), row(W_C), row(W_C), row(d), row(d), row(d)]
    if not sample:
        out_shape.append(jax.ShapeDtypeStruct((KV_RANK, t), BF16))
        out_specs.append(pl.BlockSpec((KV_RANK, tm), lambda i: (0, i)))
    return pl.pallas_call(
        functools.partial(_inproj_body, sample, tm, blocks_per_seq, d),
        grid=(nblk,),
        in_specs=in_specs,
        out_specs=out_specs,
        out_shape=out_shape,
        scratch_shapes=scratch,
        compiler_params=_cparams(("arbitrary",)),
        name="inproj_sample" if sample else "inproj_prompt",
    )(*args)


ATTN_COL_GROUP = 1024


def _attn_prompt_body(kb, qlt_ref, qpt_ref, kc_ref, kp_ref, kct_ref, o_ref, m_ref, l_ref, acc_ref):
    qi = pl.program_id(1)
    kj = pl.program_id(2)
    cols_all = N_HEADS_A * Q_BLOCK
    last = ((qi + 1) * Q_BLOCK - 1) // kb

    @pl.when(kj == 0)
    def _():
        m_ref[...] = jnp.full((1, cols_all), NEG_BIG, F32)
        l_ref[...] = jnp.zeros((1, cols_all), F32)
        acc_ref[...] = jnp.zeros((KV_RANK, cols_all), F32)

    def step(masked):
        kc = kc_ref[...]
        kp = kp_ref[...]
        kct = kct_ref[...]
        for c0 in range(0, cols_all, ATTN_COL_GROUP):
            cols = slice(c0, c0 + ATTN_COL_GROUP)
            s = _dot(kc, qlt_ref[:, cols]) + _dot(kp, qpt_ref[:, cols])
            if masked:
                kpos = kj * kb + lax.broadcasted_iota(jnp.int32, (kb, 1), 0)
                qpos = qi * Q_BLOCK + lax.broadcasted_iota(jnp.int32, (1, ATTN_COL_GROUP), 1) % Q_BLOCK
                s = jnp.where(kpos <= qpos, s, NEG_BIG)
            m_prev = m_ref[:, cols]
            m_next = jnp.maximum(m_prev, jnp.max(s, axis=0, keepdims=True))
            p = jnp.exp(s - m_next)
            alpha = jnp.exp(m_prev - m_next)
            l_ref[:, cols] = l_ref[:, cols] * alpha + jnp.sum(p, axis=0, keepdims=True)
            acc_ref[:, cols] = acc_ref[:, cols] * alpha + _dot(kct, p.astype(BF16))
            m_ref[:, cols] = m_next

    @pl.when(kj < last)
    def _():
        step(False)

    @pl.when(kj == last)
    def _():
        step(True)
        o_t = acc_ref[...] / l_ref[...]
        for hd in range(N_HEADS_A):
            o_ref[hd] = o_t[:, hd * Q_BLOCK:(hd + 1) * Q_BLOCK].T.astype(BF16)


def _attn_prompt(qlt, qpt, kcb, kpb, kct, batch, seq, kb):
    nq = seq // Q_BLOCK
    nk = seq // kb
    t = batch * seq
    cols_all = N_HEADS_A * Q_BLOCK

    def kblk(b, qi, kj):
        return b * nk + jnp.minimum(kj, ((qi + 1) * Q_BLOCK - 1) // kb)

    qmap = lambda b, qi, kj: (b * nq + qi, 0, 0)
    return pl.pallas_call(
        functools.partial(_attn_prompt_body, kb),
        grid=(batch, nq, nk),
        in_specs=[
            pl.BlockSpec((None, KV_RANK, cols_all), qmap),
            pl.BlockSpec((None, ROPE_DIM, cols_all), qmap),
            pl.BlockSpec((kb, KV_RANK), lambda b, qi, kj: (kblk(b, qi, kj), 0)),
            pl.BlockSpec((kb, ROPE_DIM), lambda b, qi, kj: (kblk(b, qi, kj), 0)),
            pl.BlockSpec((KV_RANK, kb), lambda b, qi, kj: (0, kblk(b, qi, kj))),
        ],
        out_specs=pl.BlockSpec((N_HEADS_A, Q_BLOCK, KV_RANK), lambda b, qi, kj: (0, b * nq + qi, 0)),
        out_shape=jax.ShapeDtypeStruct((N_HEADS_A, t, KV_RANK), BF16),
        scratch_shapes=[pltpu.VMEM((1, cols_all), F32), pltpu.VMEM((1, cols_all), F32),
                        pltpu.VMEM((KV_RANK, cols_all), F32)],
        compiler_params=_cparams(("arbitrary", "arbitrary", "arbitrary")),
        name="attn_prompt",
    )(qlt, qpt, kcb, kpb, kct)


SAMPLE_KEY_CHUNK = 2048


def _attn_sample_dma_body(layer, n_pages, dec_seq, page, pt_ref, qlat_ref, qpe_ref, knc_ref, knp_ref,
                          ckv_hbm, kpet_hbm, o_ref, kbuf0, kbuf1, pbuf0, pbuf1, sem):
    g = pl.program_id(0)
    ng = pl.num_programs(0)
    rows = N_HEADS_A * dec_seq
    kbufs, pbufs = (kbuf0, kbuf1), (pbuf0, pbuf1)

    def page_copies(seq, slot):
        copies = []
        for p in range(n_pages):
            pg = pt_ref[seq, p]
            copies.append(pltpu.make_async_copy(
                ckv_hbm.at[layer, pg], kbufs[slot].at[pl.ds(p * page, page), :], sem.at[0, slot]))
            copies.append(pltpu.make_async_copy(
                kpet_hbm.at[layer, pg], pbufs[slot].at[:, pl.ds(p * page, page)], sem.at[1, slot]))
        return copies

    def start_all(seq, slot):
        for c in page_copies(seq, slot):
            c.start()

    def wait_all(seq, slot):
        for c in page_copies(seq, slot):
            c.wait()

    def attend(slot):
        r0 = slot * dec_seq
        q = jnp.concatenate([qlat_ref[hd, r0:r0 + dec_seq, :] for hd in range(N_HEADS_A)], axis=0).astype(BF16)
        qp = jnp.concatenate([qpe_ref[hd, r0:r0 + dec_seq, :] for hd in range(N_HEADS_A)], axis=0).astype(BF16)
        chunks = range(0, n_pages * page, SAMPLE_KEY_CHUNK)
        ks = [kbufs[slot][c0:c0 + SAMPLE_KEY_CHUNK, :].astype(BF16) for c0 in chunks]
        ss = [_dot_nt(q, k) + _dot(qp, pbufs[slot][:, c0:c0 + SAMPLE_KEY_CHUNK].astype(BF16))
              for k, c0 in zip(ks, chunks)]
        kn = knc_ref[slot]
        qpos = lax.broadcasted_iota(jnp.int32, (rows, 1), 0) % dec_seq
        kpos = lax.broadcasted_iota(jnp.int32, (1, page), 1)
        sn = jnp.where(kpos <= qpos, _dot_nt(q, kn) + _dot(qp, knp_ref[slot]), NEG_BIG)
        m = jnp.max(sn, axis=1, keepdims=True)
        for s in ss:
            m = jnp.maximum(m, jnp.max(s, axis=1, keepdims=True))
        pn = jnp.exp(sn - m)
        l = jnp.sum(pn, axis=1, keepdims=True)
        acc = _dot(pn.astype(BF16), kn)
        for s, k in zip(ss, ks):
            p = jnp.exp(s - m)
            l = l + jnp.sum(p, axis=1, keepdims=True)
            acc = acc + _dot(p.astype(BF16), k)
        o = acc / l
        for hd in range(N_HEADS_A):
            o_ref[hd, r0:r0 + dec_seq, :] = o[hd * dec_seq:(hd + 1) * dec_seq, :]

    @pl.when(g == 0)
    def _():
        start_all(0, 0)

    start_all(2 * g + 1, 1)
    wait_all(2 * g, 0)
    attend(0)

    @pl.when(g + 1 < ng)
    def _():
        start_all(2 * g + 2, 0)

    wait_all(2 * g + 1, 1)
    attend(1)


def _attn_sample_dma(layer, qlat, qpe, cache_ckv, cache_kpe_t, page_table, knew_c, knew_pt, dec_seq):
    dec_b, n_pages = page_table.shape
    page = cache_ckv.shape[2]
    t = dec_b * dec_seq
    assert dec_b % 2 == 0 and (n_pages * page) % SAMPLE_KEY_CHUNK == 0
    pair = lambda w: pl.BlockSpec((N_HEADS_A, 2 * dec_seq, w), lambda g, pt: (0, g, 0))
    grid_spec = pltpu.PrefetchScalarGridSpec(
        num_scalar_prefetch=1,
        grid=(dec_b // 2,),
        in_specs=[pair(KV_RANK), pair(ROPE_DIM),
                  pl.BlockSpec((2, page, KV_RANK), lambda g, pt: (g, 0, 0)),
                  pl.BlockSpec((2, ROPE_DIM, page), lambda g, pt: (g, 0, 0)),
                  pl.BlockSpec(memory_space=pl.ANY), pl.BlockSpec(memory_space=pl.ANY)],
        out_specs=pair(KV_RANK),
        scratch_shapes=[pltpu.VMEM((n_pages * page, KV_RANK), F32), pltpu.VMEM((n_pages * page, KV_RANK), F32),
                        pltpu.VMEM((ROPE_DIM, n_pages * page), F32), pltpu.VMEM((ROPE_DIM, n_pages * page), F32),
                        pltpu.SemaphoreType.DMA((2, 2))],
    )
    return pl.pallas_call(
        functools.partial(_attn_sample_dma_body, layer, n_pages, dec_seq, page),
        grid_spec=grid_spec,
        out_shape=jax.ShapeDtypeStruct((N_HEADS_A, t, KV_RANK), F32),
        compiler_params=_cparams(("arbitrary",)),
        name="attn_sample",
    )(page_table, qlat, qpe, knew_c, knew_pt, cache_ckv, cache_kpe_t)


def _merge_body(x_ref, olat_ref, wuvbd_ref, bconv_ref, cmix_ref, sga_ref, sgb_ref, sgc_ref,
                wbra_ref, wbrb_ref, wbrc_ref, wout_ref, gffn_ref, wpq_ref, skeys_ref,
                x1_ref, h2t_ref, st_ref):
    ocat = jnp.concatenate([olat_ref[hd].astype(BF16) for hd in range(N_HEADS_A)], axis=1)
    a = _dot(ocat, wuvbd_ref[...]).astype(BF16)
    merged = (sga_ref[...] * _dot(a, wbra_ref[...])
              + sgb_ref[...] * _dot(bconv_ref[...], wbrb_ref[...])
              + sgc_ref[...] * _dot(cmix_ref[...], wbrc_ref[...]))
    x1 = x_ref[...] + _dot(merged.astype(BF16), wout_ref[...])
    x1_ref[...] = x1
    h2 = _rms_rows(x1, gffn_ref[...])
    h2t_ref[...] = h2.T.astype(BF16)
    q = _dot(h2.astype(BF16), wpq_ref[...]).astype(BF16)
    half = q.shape[1] // (2 * N_HEADS_P)
    for hc in range(2 * N_HEADS_P):
        st_ref[hc * N_KEYS:(hc + 1) * N_KEYS, :] = _dot_nt(skeys_ref[hc], q[:, hc * half:(hc + 1) * half])


def _merge(x, olat, bconv, cmix, sga, sgb, sgc, lw, tm):
    t, d = x.shape
    row = lambda w: pl.BlockSpec((tm, w), lambda i: (i, 0))
    col = lambda r: pl.BlockSpec((r, tm), lambda i: (0, i))
    n_s = 2 * N_HEADS_P * N_KEYS
    return pl.pallas_call(
        _merge_body,
        grid=(t // tm,),
        in_specs=[row(d), pl.BlockSpec((N_HEADS_A, tm, KV_RANK), lambda i: (0, i, 0)),
                  _const_spec(lw["w_uv_bd"].shape), row(W_B), row(W_C), row(d), row(d), row(d),
                  _const_spec(lw["w_br_a"].shape), _const_spec(lw["w_br_b"].shape),
                  _const_spec(lw["w_br_c"].shape), _const_spec(lw["w_out"].shape), _const_spec((1, d)),
                  _const_spec(lw["w_pq"].shape), _const_spec(lw["skeys"].shape)],
        out_specs=[row(d), col(d), col(n_s)],
        out_shape=[jax.ShapeDtypeStruct((t, d), F32), jax.ShapeDtypeStruct((d, t), BF16),
                   jax.ShapeDtypeStruct((n_s, t), F32)],
        compiler_params=_cparams(("arbitrary",)),
        name="merge",
    )(x, olat, lw["w_uv_bd"], bconv, cmix, sga, sgb, sgc, lw["w_br_a"], lw["w_br_b"], lw["w_br_c"],
      lw["w_out"], lw["g_ffn"], lw["w_pq"], lw["skeys"])


def _sort_network(n):
    pairs = []
    p = 1
    while p < n:
        k = p
        while k >= 1:
            for j in range(k % p, n - k, 2 * k):
                for i in range(min(k, n - j - k)):
                    if (i + j) // (2 * p) == (i + j + k) // (2 * p):
                        pairs.append((i + j, i + j + k))
            k //= 2
        p *= 2
    return pairs


def _top_values(slabs, n, out_ref, base):
    assert len(slabs) == n
    v = list(slabs)
    for i, j in _sort_network(n):
        v[i], v[j] = jnp.maximum(v[i], v[j]), jnp.minimum(v[i], v[j])
    for k in range(n):
        mx = jnp.max(v[0], axis=0, keepdims=True)
        out_ref[base + k:base + k + 1, :] = mx
        if k + 1 < n:
            hit = v[0] == mx
            for r in range(n - 1 - k):
                v[r] = jnp.where(hit, v[r + 1], v[r])


def _peer_select_body(st_ref, a0_ref, b1_ref, cthr_ref, top_ref, cand_ref):
    tn = st_ref.shape[1]
    n_half = N_HEADS_P * N_KEYS
    for hc in range(2 * N_HEADS_P):
        _top_values([st_ref[hc * N_KEYS + 8 * r:hc * N_KEYS + 8 * r + 8, :] for r in range(N_KEYS // 8)],
                    TOPK_P, top_ref, hc * TOPK_P)
    rank = lax.broadcasted_iota(jnp.int32, (TOPK_P, 1), 0)
    for hd in range(N_HEADS_P):
        s0 = top_ref[hd * TOPK_P:(hd + 1) * TOPK_P, :]
        s1 = top_ref[(N_HEADS_P + hd) * TOPK_P:(N_HEADS_P + hd + 1) * TOPK_P, :]
        half = TOPK_P // 2
        s0_lo = s0[0:half, :]
        part0 = s0 + s1[0:1, :]
        mid = [jnp.where(rank[0:half] < TOPK_P // (b + 1), s0_lo + s1[b:b + 1, :], -jnp.inf)
               for b in range(1, half)]
        tail = s0[0:1, :] + s1[half:TOPK_P, :]
        cands = jnp.concatenate([part0] + mid + [tail], axis=0)
        pad = jnp.full((half, tn), -jnp.inf, F32)
        _top_values([part0[0:half], part0[half:TOPK_P]] + mid + [tail] + [pad] * (TOPK_P - half - 2),
                    TOPK_P, cand_ref, 0)
        thr = cand_ref[TOPK_P - 1:TOPK_P, :]
        mx = cand_ref[0:1, :]
        z = jnp.sum(jnp.where(cands >= thr, jnp.exp(cands - mx), 0.0), axis=0, keepdims=True)
        tau_lo = jnp.full((half, tn), jnp.inf, F32)
        for b in range(1, half):
            tau_lo = jnp.minimum(tau_lo, jnp.where(mid[b - 1] >= thr, s1[b:b + 1, :], jnp.inf))
        tau_0 = jnp.min(jnp.where(tail >= thr, s1[half:TOPK_P, :], jnp.inf), axis=0, keepdims=True)
        tau_lo = jnp.minimum(tau_lo, jnp.where(rank[0:half] == 0, tau_0, jnp.inf))
        tau = jnp.minimum(jnp.where(part0 >= thr, s1[0:1, :], jnp.inf),
                          jnp.concatenate([tau_lo, jnp.full((half, tn), jnp.inf, F32)], axis=0))
        s0_all = st_ref[hd * N_KEYS:(hd + 1) * N_KEYS, :]
        s1_all = st_ref[n_half + hd * N_KEYS:n_half + (hd + 1) * N_KEYS, :]

        def factor1(x):
            return jnp.exp(x - s1[0:1, :]) * (0.5 / z)

        beta = factor1(tau)
        cthr = jnp.full((N_KEYS, tn), jnp.inf, F32)
        for a in reversed(range(TOPK_P)):
            cthr = jnp.where(s0_all >= s0[a:a + 1, :], beta[a:a + 1, :], cthr)
        cthr_ref[hd * N_KEYS:(hd + 1) * N_KEYS, :] = cthr
        a0_ref[hd * N_KEYS:(hd + 1) * N_KEYS, :] = jnp.exp(s0_all - s0[0:1, :])
        b1_ref[hd * N_KEYS:(hd + 1) * N_KEYS, :] = factor1(s1_all)


def _peer_select(st, tn):
    n_s, t = st.shape
    n_h = N_HEADS_P * N_KEYS
    col = lambda r: pl.BlockSpec((r, tn), lambda i: (0, i))
    return pl.pallas_call(
        _peer_select_body,
        grid=(t // tn,),
        in_specs=[col(n_s)],
        out_specs=[col(n_h), col(n_h), col(n_h)],
        out_shape=[jax.ShapeDtypeStruct((n_h, t), F32)] * 3,
        scratch_shapes=[pltpu.VMEM((2 * N_HEADS_P * TOPK_P, tn), F32), pltpu.VMEM((TOPK_P, tn), F32)],
        compiler_params=_cparams(("arbitrary",)),
        name="peer_select",
    )(st)


PEER_LANE_CHUNK = 256


def _peer_gate_block(item, at, a0_ref, b1_ref, cthr_ref, w_ref, keys_per_block, lane_chunk):
    tn = at.shape[1]
    for c0 in range(0, tn, lane_chunk):
        lanes = slice(c0, c0 + lane_chunk)
        for ii in range(keys_per_block):
            gate = None
            for hd in range(N_HEADS_P):
                b1 = b1_ref[hd * N_KEYS:(hd + 1) * N_KEYS, lanes]
                cthr = cthr_ref[hd, item, ii:ii + 1, lanes]
                a0 = a0_ref[hd, item, ii:ii + 1, lanes]
                term = a0 * jnp.where(b1 >= cthr, b1, 0.0)
                gate = term if gate is None else gate + term
            xa = at[ii * N_KEYS:(ii + 1) * N_KEYS, lanes]
            act2 = xa * (1.0 + lax.erf(xa * (2.0 ** -0.5)))
            w_ref[ii * N_KEYS:(ii + 1) * N_KEYS, lanes] = (gate * act2).astype(BF16)


def _peer_dense_body(eb, lane_chunk, h2t_ref, u_ref, vta_ref, vtb_ref, vtc_ref, a0_ref, b1_ref,
                     cthr_ref, out_ref, w0_ref, w1_ref):
    k = pl.program_id(1)
    nk = pl.num_programs(1)
    keys_per_block = eb // N_KEYS
    gate_args = (a0_ref, b1_ref, cthr_ref)

    @pl.when(k == 0)
    def _():
        out_ref[...] = jnp.zeros(out_ref.shape, F32)
        w1_ref[...] = jnp.zeros(w1_ref.shape, BF16)

    h2t = h2t_ref[...]
    at0 = _dot(u_ref[0:eb, :], h2t)
    out_ref[...] += _dot(vta_ref[...], w1_ref[...])
    _peer_gate_block(0, at0, *gate_args, w0_ref, keys_per_block, lane_chunk)
    at1 = _dot(u_ref[eb:2 * eb, :], h2t)
    out_ref[...] += _dot(vtb_ref[...], w0_ref[...])
    _peer_gate_block(1, at1, *gate_args, w1_ref, keys_per_block, lane_chunk)

    @pl.when(k == nk - 1)
    def _():
        out_ref[...] += _dot(vtc_ref[...], w1_ref[...])


def _peer_dense(h2t, u_bf, vt_bf, a0, b1, cthr, tn, eb):
    d, t = h2t.shape
    n_e = u_bf.shape[0]
    n_blocks = n_e // eb
    assert eb == 8 * N_KEYS
    assert n_blocks % 2 == 0
    col = lambda r: pl.BlockSpec((r, tn), lambda i, k: (0, i))
    n_h = N_HEADS_P * N_KEYS
    keys_per_block = eb // N_KEYS
    by_block = lambda a: a.reshape(N_HEADS_P, n_blocks, keys_per_block, t)
    blk_rows = pl.BlockSpec((N_HEADS_P, 2, keys_per_block, tn), lambda i, k: (0, k, 0, i))
    return pl.pallas_call(
        functools.partial(_peer_dense_body, eb, PEER_LANE_CHUNK),
        grid=(t // tn, n_blocks // 2),
        in_specs=[col(d), pl.BlockSpec((2 * eb, d), lambda i, k: (k, 0)),
                  pl.BlockSpec((d, eb), lambda i, k: (0, jnp.maximum(2 * k - 1, 0))),
                  pl.BlockSpec((d, eb), lambda i, k: (0, 2 * k)),
                  pl.BlockSpec((d, eb), lambda i, k: (0, n_blocks - 1)),
                  blk_rows, col(n_h), blk_rows],
        out_specs=col(d),
        out_shape=jax.ShapeDtypeStruct((d, t), F32),
        scratch_shapes=[pltpu.VMEM((eb, tn), BF16), pltpu.VMEM((eb, tn), BF16)],
        compiler_params=_cparams(("arbitrary", "arbitrary")),
        name="peer_dense",
    )(h2t, u_bf, vt_bf, vt_bf, vt_bf, by_block(a0), b1, by_block(cthr))


def _table_prep_body(u_ref, v_ref, ub_ref, vt_ref):
    ub_ref[...] = u_ref[...].astype(BF16)
    vt_ref[...] = v_ref[...].T.astype(BF16)


def _table_prep(layer, u_tab, v_tab, rows):
    _, n_e, d = u_tab.shape
    src = pl.BlockSpec((None, rows, d), lambda e: (layer, e, 0))
    return pl.pallas_call(
        _table_prep_body,
        grid=(n_e // rows,),
        in_specs=[src, src],
        out_specs=[pl.BlockSpec((rows, d), lambda e: (e, 0)), pl.BlockSpec((d, rows), lambda e: (0, e))],
        out_shape=[jax.ShapeDtypeStruct((n_e, d), BF16), jax.ShapeDtypeStruct((d, n_e), BF16)],
        compiler_params=_cparams(("arbitrary",)),
        name="table_prep",
    )(u_tab, v_tab)


def _ple_body(last, x1_ref, pt_ref, gple_ref, wpg_ref, p_ref, wple_ref, *rest):
    if last:
        gfin_ref, x3_ref, y_ref = rest
    else:
        (x3_ref,) = rest
    x2 = x1_ref[...] + pt_ref[...].T
    hp = _rms_rows(x2, gple_ref[...]).astype(BF16)
    gate = jax.nn.sigmoid(_dot(hp, wpg_ref[...]))
    x3 = x2 + gate * _dot(p_ref[...].astype(BF16), wple_ref[...])
    x3_ref[...] = x3
    if last:
        y_ref[...] = _rms_rows(x3, gfin_ref[...])


def _ple(x1, peer_t, col_off, p_i, lw, g_final, last, tm):
    t, d = x1.shape
    row = lambda w: pl.BlockSpec((tm, w), lambda i: (i, 0))
    in_specs = [row(d), pl.BlockSpec((d, tm), lambda i: (0, i + col_off)), _const_spec((1, d)),
                _const_spec(lw["w_pg"].shape), row(p_i.shape[1]), _const_spec(lw["w_ple"].shape)]
    args = [x1, peer_t, lw["g_ple"], lw["w_pg"], p_i, lw["w_ple"]]
    out_specs = [row(d)]
    out_shape = [jax.ShapeDtypeStruct((t, d), F32)]
    if last:
        in_specs.append(_const_spec((1, d)))
        args.append(g_final)
        out_specs.append(row(d))
        out_shape.append(jax.ShapeDtypeStruct((t, d), F32))
    return pl.pallas_call(
        functools.partial(_ple_body, last),
        grid=(t // tm,),
        in_specs=in_specs,
        out_specs=out_specs,
        out_shape=out_shape,
        compiler_params=_cparams(("arbitrary",)),
        name="ple_last" if last else "ple",
    )(*args)


def _rope_tables(pos):
    inv = 1.0 / (ROPE_THETA ** (jnp.arange(0, ROPE_DIM, 2, dtype=F32) / ROPE_DIM))
    ang = pos.astype(F32)[:, None] * inv[None, :]
    cos, sin = jnp.cos(ang), jnp.sin(ang)
    cosk = jnp.concatenate([cos, cos], axis=1)
    sink = jnp.concatenate([-sin, sin], axis=1)
    return {"cosk": cosk, "sink": sink, "cosq": jnp.tile(cosk, (1, N_HEADS_A)),
            "sinq": jnp.tile(sink, (1, N_HEADS_A))}


def _swap_halves(w):
    half = w.shape[-1] // 2
    return jnp.concatenate([w[..., half:], w[..., :half]], axis=-1)


def _block_diag(blocks):
    n, r, c = blocks.shape
    eye = jnp.eye(n, dtype=blocks.dtype)
    return (eye[:, None, :, None] * blocks[:, :, None, :]).reshape(n * r, n * c)


def _layer_weights(i, d, dec_seq, g_attn, w_in, g_q, g_kv, w_uq, w_uk, w_uv, w_conv, g_v, b_v, w_s, b_s,
                   w_br_a, w_br_b, w_br_c, w_out, g_ffn, w_pq, sub_keys, u_tab, v_tab, g_ple, w_pg, w_ple):
    bf = lambda a: a.astype(BF16)
    wi = w_in[i]
    o_kpe = Q_RANK + KV_RANK
    o_cb = o_kpe + ROPE_DIM
    o_g = o_cb + 3 * W_B + 2 * W_C
    w_kpe = wi[:, o_kpe:o_cb]
    w_small = jnp.concatenate([wi[:, :o_kpe], wi[:, o_cb:o_g], w_kpe, _swap_halves(w_kpe)], axis=1)
    uq = w_uq[i]
    uq_pe = uq[:, :, NOPE_DIM:]
    w_uq_p = jnp.concatenate([uq[:, :, :NOPE_DIM].reshape(Q_RANK, -1), uq_pe.reshape(Q_RANK, -1),
                              _swap_halves(uq_pe).reshape(Q_RANK, -1)], axis=1)
    ws = jnp.tril(w_s[i])
    ws_small = jnp.tril(w_s[i][:, :dec_seq, :dec_seq])
    eye = jnp.eye(CHUNK // dec_seq, dtype=F32)
    ms_s = jnp.einsum("ab,gts->gatbs", eye, ws_small).reshape(N_GROUPS_C, CHUNK, CHUNK)
    gw = W_C // N_GROUPS_C
    bs_p = jnp.repeat(b_s[i].T, gw, axis=1)
    bs_s = jnp.tile(bs_p[:dec_seq], (CHUNK // dec_seq, 1))
    return {
        "g_attn": g_attn[i][None], "w_small": bf(w_small), "w_gates": bf(wi[:, o_g:]),
        "g_q": g_q[i][None], "g_kv": g_kv[i][None], "w_uq": bf(w_uq_p),
        "w_uk_bd": bf(_block_diag(jnp.transpose(w_uk[i], (1, 2, 0)))),
        "w_uv_bd": bf(_block_diag(jnp.transpose(w_uv[i], (1, 0, 2)))),
        "w_conv": w_conv[i], "g_v": g_v[i][None], "b_v": b_v[i][None],
        "ms_p": bf(ws), "ms_s": bf(ms_s), "bs_p": bs_p, "bs_s": bs_s,
        "w_br_a": bf(w_br_a[i]), "w_br_b": bf(w_br_b[i]), "w_br_c": bf(w_br_c[i]), "w_out": bf(w_out[i]),
        "g_ffn": g_ffn[i][None], "w_pq": bf(jnp.transpose(w_pq[i], (0, 2, 1, 3)).reshape(d, -1)),
        "skeys": bf(jnp.transpose(sub_keys[i], (1, 0, 2, 3)).reshape(2 * N_HEADS_P, N_KEYS, -1)),
        "g_ple": g_ple[i][None], "w_pg": bf(w_pg[i]), "w_ple": bf(w_ple[i]),
    }


def _pick_tile(n, prefs):
    for p in prefs:
        if n % p == 0:
            return p
    raise ValueError(f"no tile for {n}")


def kernel(x_prompt, x_sample, p_prompt, p_sample, cache_ckv, cache_kpe, state_conv, page_table,
           g_attn, w_in, g_q, g_kv, w_uq, w_uk, w_uv, w_conv, g_v, b_v, w_s, b_s,
           w_br_a, w_br_b, w_br_c, w_out, g_ffn, w_pq, sub_keys, u_tab, v_tab,
           g_ple, w_pg, w_ple, g_final):
    batch, seq, d = x_prompt.shape
    dec_b, dec_seq, _ = x_sample.shape
    depth = w_in.shape[0]
    n_pages = page_table.shape[1]
    page = cache_ckv.shape[2]
    past_len = n_pages * page
    tp, ts = batch * seq, dec_b * dec_seq
    assert seq % CHUNK == 0 and CHUNK % dec_seq == 0 and ts % CHUNK == 0 and dec_seq == 8
    assert page == CHUNK

    tm_p = _pick_tile(seq, (256, 128))
    tm_s = _pick_tile(ts, (256, 128))
    kb = _pick_tile(seq, (512, 256, 128))
    tn_p = _pick_tile(tp, (512, 256, 128))
    tn_s = _pick_tile(ts, (512, 256, 128))
    eb = 1024

    tabs_p = _rope_tables(jnp.arange(seq, dtype=jnp.int32))
    tabs_s = _rope_tables(past_len + (jnp.arange(tm_s, dtype=jnp.int32) % dec_seq))
    g_fin = g_final[None]
    cache_kpe_t = jnp.swapaxes(cache_kpe, 2, 3)

    xp = x_prompt.reshape(tp, d)
    xs = x_sample.reshape(ts, d)
    outs = {k: [] for k in ("ckv_p", "kpe_p", "ckv_s", "kpe_s", "conv_p", "conv_s", "v_s")}
    y_p = y_s = None
    for i in range(depth):
        lw = _layer_weights(i, d, dec_seq, g_attn, w_in, g_q, g_kv, w_uq, w_uk, w_uv, w_conv, g_v, b_v,
                            w_s, b_s, w_br_a, w_br_b, w_br_c, w_out, g_ffn, w_pq, sub_keys, u_tab, v_tab,
                            g_ple, w_pg, w_ple)
        last = i == depth - 1

        (qlt_p, qpt_p, ckv_p, kpe_p, kcb_p, kpb_p, bconv_p, xc_p, cmix_p, _vn_p, sga_p, sgb_p, sgc_p,
         kct_p) = _inproj(xp, lw, tabs_p, False, seq, None, tm_p)
        olat_p = _attn_prompt(qlt_p, qpt_p, kcb_p, kpb_p, kct_p, batch, seq, kb)
        x1_p, h2t_p, st_p = _merge(xp, olat_p, bconv_p, cmix_p, sga_p, sgb_p, sgc_p, lw, tm_p)

        st_c = state_conv[i]
        zero_row = jnp.zeros((dec_b, 1, W_B), F32)
        f1 = jnp.concatenate([st_c[:, 1:2], jnp.zeros((dec_b, dec_seq - 1, W_B), F32)], axis=1).reshape(ts, W_B)
        f2 = jnp.concatenate([st_c[:, 0:1], st_c[:, 1:2], jnp.zeros((dec_b, dec_seq - 2, W_B), F32)],
                             axis=1).reshape(ts, W_B)
        del zero_row
        (qlat_s, qpe_s, ckv_s, kpe_s, kcb_s, kpb_s, bconv_s, xc_s, cmix_s, vn_s, sga_s, sgb_s, sgc_s) = _inproj(
            xs, lw, tabs_s, True, dec_seq, (f1, f2), tm_s)
        knew_c = jnp.pad(kcb_s.reshape(dec_b, dec_seq, KV_RANK), ((0, 0), (0, page - dec_seq), (0, 0)))
        knew_p = jnp.pad(jnp.swapaxes(kpb_s.reshape(dec_b, dec_seq, ROPE_DIM), 1, 2),
                         ((0, 0), (0, 0), (0, page - dec_seq)))
        olat_s = _attn_sample_dma(i, qlat_s, qpe_s, cache_ckv, cache_kpe_t, page_table, knew_c, knew_p, dec_seq)
        x1_s, h2t_s, st_s = _merge(xs, olat_s, bconv_s, cmix_s, sga_s, sgb_s, sgc_s, lw, tm_s)

        u_bf, vt_bf = _table_prep(i, u_tab, v_tab, eb)
        peer_p = _peer_dense(h2t_p, u_bf, vt_bf, *_peer_select(st_p, tn_p), tn_p, eb)
        peer_s = _peer_dense(h2t_s, u_bf, vt_bf, *_peer_select(st_s, tn_s), tn_s, eb)

        res_p = _ple(x1_p, peer_p, 0, p_prompt[i].reshape(tp, -1), lw, g_fin, last, tm_p)
        res_s = _ple(x1_s, peer_s, 0, p_sample[i].reshape(ts, -1), lw, g_fin, last, tm_s)
        xp, xs = res_p[0], res_s[0]
        if last:
            y_p, y_s = res_p[1], res_s[1]

        outs["ckv_p"].append(ckv_p.reshape(batch, seq, KV_RANK))
        outs["kpe_p"].append(kpe_p.reshape(batch, seq, ROPE_DIM))
        outs["ckv_s"].append(ckv_s.reshape(dec_b, dec_seq, KV_RANK))
        outs["kpe_s"].append(kpe_s.reshape(dec_b, dec_seq, ROPE_DIM))
        outs["conv_p"].append(xc_p.reshape(batch, seq, W_B)[:, seq - (CONV_W - 1):])
        outs["conv_s"].append(xc_s.reshape(dec_b, dec_seq, W_B)[:, dec_seq - (CONV_W - 1):])
        outs["v_s"].append(vn_s.reshape(dec_b, dec_seq, W_C))

    return (y_p.reshape(batch, seq, d), y_s.reshape(dec_b, dec_seq, d),
            jnp.stack(outs["ckv_p"]), jnp.stack(outs["kpe_p"]), jnp.stack(outs["ckv_s"]),
            jnp.stack(outs["kpe_s"]), jnp.stack(outs["conv_p"]), jnp.stack(outs["conv_s"]),
            jnp.stack(outs["v_s"]))
```

```python
import functools

import jax
import jax.numpy as jnp
from jax import lax
from jax.experimental import pallas as pl
from jax.experimental.pallas import tpu as pltpu

F32 = jnp.float32
BF16 = jnp.bfloat16

N_HEADS_A = 8
NOPE_DIM = 64
ROPE_DIM = 32
V_DIM = 64
Q_RANK = 768
KV_RANK = 256
ROPE_THETA = 10000.0
Q_BLOCK = 128
MLA_SCALE = (NOPE_DIM + ROPE_DIM) ** -0.5
W_B = 512
CONV_W = 3
W_C = 512
N_GROUPS_C = 4
CHUNK = 128
N_KEYS = 128
N_HEADS_P = 8
TOPK_P = 16
EPS = 1e-6

NEG_BIG = -1e30
VMEM_LIMIT_BYTES = 56 * 1024 * 1024

_OFF_CQ = 0
_OFF_CKV = _OFF_CQ + Q_RANK
_OFF_CB = _OFF_CKV + KV_RANK
_OFF_CC = _OFF_CB + W_B
_OFF_CH = _OFF_CC + W_B
_OFF_U = _OFF_CH + W_B
_OFF_V = _OFF_U + W_C
_OFF_KPE = _OFF_V + W_C
_OFF_KPE_SW = _OFF_KPE + ROPE_DIM
_N_SMALL = _OFF_KPE_SW + ROPE_DIM


def _cparams(semantics):
    return pltpu.CompilerParams(dimension_semantics=semantics, vmem_limit_bytes=VMEM_LIMIT_BYTES)


def _rms_rows(x, g):
    return x * lax.rsqrt(jnp.mean(x * x, axis=-1, keepdims=True) + EPS) * g


def _dot(a, b):
    return jnp.dot(a, b, preferred_element_type=F32)


def _dot_nt(a, b):
    return lax.dot_general(a, b, (((1,), (1,)), ((), ())), preferred_element_type=F32)


def _const_spec(shape):
    nd = len(shape)
    return pl.BlockSpec(shape, lambda *_: (0,) * nd)


def _inproj_body(sample, tm, blocks_per_seq, d_model, *refs):
    if sample:
        (x_ref, gattn_ref, wsm_ref, wg_ref, gq_ref, gkv_ref, wuq_ref, wukbd_ref, cosk_ref, sink_ref,
         cosq_ref, sinq_ref, wconv_ref, gv_ref, bv_ref, ms_ref, bs_ref, f1_ref, f2_ref,
         qlat_ref, qpe_ref, ckv_ref, kpe_ref, kcb_ref, kpb_ref, bconv_ref, xc_ref, cmix_ref, vn_ref,
         sga_ref, sgb_ref, sgc_ref) = refs
        carry_ref = kct_ref = None
    else:
        (x_ref, gattn_ref, wsm_ref, wg_ref, gq_ref, gkv_ref, wuq_ref, wukbd_ref, cosk_ref, sink_ref,
         cosq_ref, sinq_ref, wconv_ref, gv_ref, bv_ref, ms_ref, bs_ref,
         qlat_ref, qpe_ref, ckv_ref, kpe_ref, kcb_ref, kpb_ref, bconv_ref, xc_ref, cmix_ref, vn_ref,
         sga_ref, sgb_ref, sgc_ref, kct_ref, carry_ref) = refs

    x = x_ref[...]
    h = _rms_rows(x, gattn_ref[...]).astype(BF16)

    def proj(a, b):
        return _dot(h, wsm_ref[:, a:b])

    cq = proj(_OFF_CQ, _OFF_CKV)
    cqn = _rms_rows(cq, gq_ref[...]).astype(BF16)
    qall = _dot(cqn, wuq_ref[...])
    n_nope = N_HEADS_A * NOPE_DIM
    n_pe = N_HEADS_A * ROPE_DIM
    qpe = (qall[:, n_nope:n_nope + n_pe] * cosq_ref[...]
           + qall[:, n_nope + n_pe:n_nope + 2 * n_pe] * sinq_ref[...]) * MLA_SCALE
    qlat = _dot(qall[:, :n_nope].astype(BF16), wukbd_ref[...]) * MLA_SCALE
    if sample:
        for hd in range(N_HEADS_A):
            qlat_ref[hd] = qlat[:, hd * KV_RANK:(hd + 1) * KV_RANK]
            qpe_ref[hd] = qpe[:, hd * ROPE_DIM:(hd + 1) * ROPE_DIM]
    else:
        for blk in range(tm // Q_BLOCK):
            r0 = blk * Q_BLOCK
            qpe_t = qpe[r0:r0 + Q_BLOCK, :].T
            for hd in range(N_HEADS_A):
                cols = slice(hd * Q_BLOCK, (hd + 1) * Q_BLOCK)
                qlat_ref[blk, :, cols] = qlat[r0:r0 + Q_BLOCK, hd * KV_RANK:(hd + 1) * KV_RANK].T.astype(BF16)
                qpe_ref[blk, :, cols] = qpe_t[hd * ROPE_DIM:(hd + 1) * ROPE_DIM, :].astype(BF16)

    ckv = _rms_rows(proj(_OFF_CKV, _OFF_CB), gkv_ref[...])
    ckv_ref[...] = ckv
    kcb_ref[...] = ckv.astype(BF16)
    if not sample:
        kct_ref[...] = ckv.T.astype(BF16)
    kpe = proj(_OFF_KPE, _OFF_KPE_SW) * cosk_ref[...] + proj(_OFF_KPE_SW, _N_SMALL) * sink_ref[...]
    kpe_ref[...] = kpe
    kpb_ref[...] = kpe.astype(BF16)

    cb = proj(_OFF_CB, _OFF_CC)
    xc = proj(_OFF_CC, _OFF_CH) * proj(_OFF_CH, _OFF_U)
    xc_ref[...] = xc
    rows = lax.broadcasted_iota(jnp.int32, (tm, 1), 0)
    r1 = pltpu.roll(xc, 1, 0)
    r2 = pltpu.roll(xc, 2, 0)
    if sample:
        pos = rows % 8
        m1 = jnp.where(pos == 0, f1_ref[...], r1)
        m2 = jnp.where(pos < 2, f2_ref[...], r2)
    else:
        first = (pl.program_id(0) % blocks_per_seq) == 0
        prev = jnp.where(first, 0.0, carry_ref[...])
        p6 = prev[6:7, :]
        p7 = prev[7:8, :]
        m1 = jnp.where(rows == 0, p7, r1)
        m2 = jnp.where(rows == 0, p6, jnp.where(rows == 1, p7, r2))
        carry_ref[...] = xc[tm - 8:tm, :]
    y = m2 * wconv_ref[0:1, :] + m1 * wconv_ref[1:2, :] + xc * wconv_ref[2:3, :]
    bconv_ref[...] = (cb * y).astype(BF16)

    u = proj(_OFF_U, _OFF_V)
    v = proj(_OFF_V, _OFF_KPE)
    vc = v - jnp.mean(v, axis=-1, keepdims=True)
    vn = vc * lax.rsqrt(jnp.mean(vc * vc, axis=-1, keepdims=True) + EPS) * gv_ref[...] + bv_ref[...]
    vn_ref[...] = vn
    vnb = vn.astype(BF16)
    gw = W_C // N_GROUPS_C
    for r0 in range(0, tm, CHUNK):
        for g in range(N_GROUPS_C):
            mix = _dot(ms_ref[g], vnb[r0:r0 + CHUNK, g * gw:(g + 1) * gw]) + bs_ref[:, g * gw:(g + 1) * gw]
            cmix_ref[r0:r0 + CHUNK, g * gw:(g + 1) * gw] = (
                u[r0:r0 + CHUNK, g * gw:(g + 1) * gw] * mix).astype(BF16)

    sga_ref[...] = jax.nn.sigmoid(_dot(h, wg_ref[:, 0:d_model]))
    sgb_ref[...] = jax.nn.sigmoid(_dot(h, wg_ref[:, d_model:2 * d_model]))
    sgc_ref[...] = jax.nn.sigmoid(_dot(h, wg_ref[:, 2 * d_model:3 * d_model]))


def _inproj(x, lw, tabs, sample, seq_len, conv_fix, tm):
    t, d = x.shape
    nblk = t // tm
    blocks_per_seq = max(seq_len // tm, 1)
    row = lambda w: pl.BlockSpec((tm, w), lambda i: (i, 0))
    if sample:
        tab_spec = lambda w: pl.BlockSpec((tm, w), lambda i: (0, 0))
    else:
        tab_spec = lambda w: pl.BlockSpec((tm, w), lambda i: (i % blocks_per_seq, 0))
    n_pe = N_HEADS_A * ROPE_DIM
    in_specs = [
        row(d), _const_spec((1, d)), _const_spec(lw["w_small"].shape), _const_spec(lw["w_gates"].shape),
        _const_spec((1, Q_RANK)), _const_spec((1, KV_RANK)), _const_spec(lw["w_uq"].shape),
        _const_spec(lw["w_uk_bd"].shape),
        tab_spec(ROPE_DIM), tab_spec(ROPE_DIM), tab_spec(n_pe), tab_spec(n_pe),
        _const_spec((CONV_W, W_B)), _const_spec((1, W_C)), _const_spec((1, W_C)),
        _const_spec((N_GROUPS_C, CHUNK, CHUNK)), _const_spec((CHUNK, W_C)),
    ]
    args = [x, lw["g_attn"], lw["w_small"], lw["w_gates"], lw["g_q"], lw["g_kv"], lw["w_uq"], lw["w_uk_bd"],
            tabs["cosk"], tabs["sink"], tabs["cosq"], tabs["sinq"], lw["w_conv"], lw["g_v"], lw["b_v"],
            lw["ms_s"] if sample else lw["ms_p"], lw["bs_s"] if sample else lw["bs_p"]]
    scratch = []
    if sample:
        in_specs += [row(W_B), row(W_B)]
        args += [conv_fix[0], conv_fix[1]]
    else:
        scratch = [pltpu.VMEM((8, W_B), F32)]
    n_qcols = N_HEADS_A * Q_BLOCK
    if sample:
        q_shapes = [jax.ShapeDtypeStruct((N_HEADS_A, t, KV_RANK), F32),
                    jax.ShapeDtypeStruct((N_HEADS_A, t, ROPE_DIM), F32)]
        q_specs = [pl.BlockSpec((N_HEADS_A, tm, KV_RANK), lambda i: (0, i, 0)),
                   pl.BlockSpec((N_HEADS_A, tm, ROPE_DIM), lambda i: (0, i, 0))]
    else:
        q_shapes = [jax.ShapeDtypeStruct((t // Q_BLOCK, KV_RANK, n_qcols), BF16),
                    jax.ShapeDtypeStruct((t // Q_BLOCK, ROPE_DIM, n_qcols), BF16)]
        q_specs = [pl.BlockSpec((tm // Q_BLOCK, KV_RANK, n_qcols), lambda i: (i, 0, 0)),
                   pl.BlockSpec((tm // Q_BLOCK, ROPE_DIM, n_qcols), lambda i: (i, 0, 0))]
    out_shape = q_shapes + [
        jax.ShapeDtypeStruct((t, KV_RANK), F32),
        jax.ShapeDtypeStruct((t, ROPE_DIM), F32),
        jax.ShapeDtypeStruct((t, KV_RANK), BF16),
        jax.ShapeDtypeStruct((t, ROPE_DIM), BF16),
        jax.ShapeDtypeStruct((t, W_B), BF16),
        jax.ShapeDtypeStruct((t, W_B), F32),
        jax.ShapeDtypeStruct((t, W_C), BF16),
        jax.ShapeDtypeStruct((t, W_C), F32),
        jax.ShapeDtypeStruct((t, d), F32),
        jax.ShapeDtypeStruct((t, d), F32),
        jax.ShapeDtypeStruct((t, d), F32),
    ]
    out_specs = q_specs + [row(KV_RANK), row(ROPE_DIM), row(KV_RANK),
                           row(ROPE_DIM), row(W_B), row(W_B), row(W_C), row(W_C), row(d), row(d), row(d)]
    if not sample:
        out_shape.append(jax.ShapeDtypeStruct((KV_RANK, t), BF16))
        out_specs.append(pl.BlockSpec((KV_RANK, tm), lambda i: (0, i)))
    return pl.pallas_call(
        functools.partial(_inproj_body, sample, tm, blocks_per_seq, d),
        grid=(nblk,),
        in_specs=in_specs,
        out_specs=out_specs,
        out_shape=out_shape,
        scratch_shapes=scratch,
        compiler_params=_cparams(("arbitrary",)),
        name="inproj_sample" if sample else "inproj_prompt",
    )(*args)


ATTN_COL_GROUP = 1024


def _attn_prompt_body(kb, qlt_ref, qpt_ref, kc_ref, kp_ref, kct_ref, o_ref, m_ref, l_ref, acc_ref):
    qi = pl.program_id(1)
    kj = pl.program_id(2)
    cols_all = N_HEADS_A * Q_BLOCK
    last = ((qi + 1) * Q_BLOCK - 1) // kb

    @pl.when(kj == 0)
    def _():
        m_ref[...] = jnp.full((1, cols_all), NEG_BIG, F32)
        l_ref[...] = jnp.zeros((1, cols_all), F32)
        acc_ref[...] = jnp.zeros((KV_RANK, cols_all), F32)

    def step(masked):
        kc = kc_ref[...]
        kp = kp_ref[...]
        kct = kct_ref[...]
        for c0 in range(0, cols_all, ATTN_COL_GROUP):
            cols = slice(c0, c0 + ATTN_COL_GROUP)
            s = _dot(kc, qlt_ref[:, cols]) + _dot(kp, qpt_ref[:, cols])
            if masked:
                kpos = kj * kb + lax.broadcasted_iota(jnp.int32, (kb, 1), 0)
                qpos = qi * Q_BLOCK + lax.broadcasted_iota(jnp.int32, (1, ATTN_COL_GROUP), 1) % Q_BLOCK
                s = jnp.where(kpos <= qpos, s, NEG_BIG)
            m_prev = m_ref[:, cols]
            m_next = jnp.maximum(m_prev, jnp.max(s, axis=0, keepdims=True))
            p = jnp.exp(s - m_next)
            alpha = jnp.exp(m_prev - m_next)
            l_ref[:, cols] = l_ref[:, cols] * alpha + jnp.sum(p, axis=0, keepdims=True)
            acc_ref[:, cols] = acc_ref[:, cols] * alpha + _dot(kct, p.astype(BF16))
            m_ref[:, cols] = m_next

    @pl.when(kj < last)
    def _():
        step(False)

    @pl.when(kj == last)
    def _():
        step(True)
        o_t = acc_ref[...] / l_ref[...]
        for hd in range(N_HEADS_A):
            o_ref[hd] = o_t[:, hd * Q_BLOCK:(hd + 1) * Q_BLOCK].T.astype(BF16)


def _attn_prompt(qlt, qpt, kcb, kpb, kct, batch, seq, kb):
    nq = seq // Q_BLOCK
    nk = seq // kb
    t = batch * seq
    cols_all = N_HEADS_A * Q_BLOCK

    def kblk(b, qi, kj):
        return b * nk + jnp.minimum(kj, ((qi + 1) * Q_BLOCK - 1) // kb)

    qmap = lambda b, qi, kj: (b * nq + qi, 0, 0)
    return pl.pallas_call(
        functools.partial(_attn_prompt_body, kb),
        grid=(batch, nq, nk),
        in_specs=[
            pl.BlockSpec((None, KV_RANK, cols_all), qmap),
            pl.BlockSpec((None, ROPE_DIM, cols_all), qmap),
            pl.BlockSpec((kb, KV_RANK), lambda b, qi, kj: (kblk(b, qi, kj), 0)),
            pl.BlockSpec((kb, ROPE_DIM), lambda b, qi, kj: (kblk(b, qi, kj), 0)),
            pl.BlockSpec((KV_RANK, kb), lambda b, qi, kj: (0, kblk(b, qi, kj))),
        ],
        out_specs=pl.BlockSpec((N_HEADS_A, Q_BLOCK, KV_RANK), lambda b, qi, kj: (0, b * nq + qi, 0)),
        out_shape=jax.ShapeDtypeStruct((N_HEADS_A, t, KV_RANK), BF16),
        scratch_shapes=[pltpu.VMEM((1, cols_all), F32), pltpu.VMEM((1, cols_all), F32),
                        pltpu.VMEM((KV_RANK, cols_all), F32)],
        compiler_params=_cparams(("arbitrary", "arbitrary", "arbitrary")),
        name="attn_prompt",
    )(qlt, qpt, kcb, kpb, kct)


SAMPLE_KEY_CHUNK = 4096


def _attn_sample_dma_body(layer, n_pages, dec_seq, page, pt_ref, qlat_ref, qpe_ref, knc_ref, knp_ref,
                          ckv_hbm, kpet_hbm, o_ref, kbuf0, kbuf1, pbuf0, pbuf1, sem):
    g = pl.program_id(0)
    ng = pl.num_programs(0)
    rows = N_HEADS_A * dec_seq
    kbufs, pbufs = (kbuf0, kbuf1), (pbuf0, pbuf1)

    def page_copies(seq, slot):
        copies = []
        for p in range(n_pages):
            pg = pt_ref[seq, p]
            copies.append(pltpu.make_async_copy(
                ckv_hbm.at[layer, pg], kbufs[slot].at[pl.ds(p * page, page), :], sem.at[0, slot]))
            copies.append(pltpu.make_async_copy(
                kpet_hbm.at[layer, pg], pbufs[slot].at[:, pl.ds(p * page, page)], sem.at[1, slot]))
        return copies

    def start_all(seq, slot):
        for c in page_copies(seq, slot):
            c.start()

    def wait_all(seq, slot):
        for c in page_copies(seq, slot):
            c.wait()

    def attend(slot):
        r0 = slot * dec_seq
        q = jnp.concatenate([qlat_ref[hd, r0:r0 + dec_seq, :] for hd in range(N_HEADS_A)], axis=0).astype(BF16)
        qp = jnp.concatenate([qpe_ref[hd, r0:r0 + dec_seq, :] for hd in range(N_HEADS_A)], axis=0).astype(BF16)
        chunks = range(0, n_pages * page, SAMPLE_KEY_CHUNK)
        ks = [kbufs[slot][c0:c0 + SAMPLE_KEY_CHUNK, :].astype(BF16) for c0 in chunks]
        ss = [_dot_nt(q, k) + _dot(qp, pbufs[slot][:, c0:c0 + SAMPLE_KEY_CHUNK].astype(BF16))
              for k, c0 in zip(ks, chunks)]
        kn = knc_ref[slot]
        qpos = lax.broadcasted_iota(jnp.int32, (rows, 1), 0) % dec_seq
        kpos = lax.broadcasted_iota(jnp.int32, (1, page), 1)
        sn = jnp.where(kpos <= qpos, _dot_nt(q, kn) + _dot(qp, knp_ref[slot]), NEG_BIG)
        m = jnp.max(sn, axis=1, keepdims=True)
        for s in ss:
            m = jnp.maximum(m, jnp.max(s, axis=1, keepdims=True))
        pn = jnp.exp(sn - m)
        l = jnp.sum(pn, axis=1, keepdims=True)
        acc = _dot(pn.astype(BF16), kn)
        for s, k in zip(ss, ks):
            p = jnp.exp(s - m)
            l = l + jnp.sum(p, axis=1, keepdims=True)
            acc = acc + _dot(p.astype(BF16), k)
        o = acc / l
        for hd in range(N_HEADS_A):
            o_ref[hd, r0:r0 + dec_seq, :] = o[hd * dec_seq:(hd + 1) * dec_seq, :]

    @pl.when(g == 0)
    def _():
        start_all(0, 0)

    start_all(2 * g + 1, 1)
    wait_all(2 * g, 0)
    attend(0)

    @pl.when(g + 1 < ng)
    def _():
        start_all(2 * g + 2, 0)

    wait_all(2 * g + 1, 1)
    attend(1)


def _attn_sample_dma(layer, qlat, qpe, cache_ckv, cache_kpe_t, page_table, knew_c, knew_pt, dec_seq):
    dec_b, n_pages = page_table.shape
    page = cache_ckv.shape[2]
    t = dec_b * dec_seq
    assert dec_b % 2 == 0 and (n_pages * page) % SAMPLE_KEY_CHUNK == 0
    pair = lambda w: pl.BlockSpec((N_HEADS_A, 2 * dec_seq, w), lambda g, pt: (0, g, 0))
    grid_spec = pltpu.PrefetchScalarGridSpec(
        num_scalar_prefetch=1,
        grid=(dec_b // 2,),
        in_specs=[pair(KV_RANK), pair(ROPE_DIM),
                  pl.BlockSpec((2, page, KV_RANK), lambda g, pt: (g, 0, 0)),
                  pl.BlockSpec((2, ROPE_DIM, page), lambda g, pt: (g, 0, 0)),
                  pl.BlockSpec(memory_space=pl.ANY), pl.BlockSpec(memory_space=pl.ANY)],
        out_specs=pair(KV_RANK),
        scratch_shapes=[pltpu.VMEM((n_pages * page, KV_RANK), F32), pltpu.VMEM((n_pages * page, KV_RANK), F32),
                        pltpu.VMEM((ROPE_DIM, n_pages * page), F32), pltpu.VMEM((ROPE_DIM, n_pages * page), F32),
                        pltpu.SemaphoreType.DMA((2, 2))],
    )
    return pl.pallas_call(
        functools.partial(_attn_sample_dma_body, layer, n_pages, dec_seq, page),
        grid_spec=grid_spec,
        out_shape=jax.ShapeDtypeStruct((N_HEADS_A, t, KV_RANK), F32),
        compiler_params=_cparams(("arbitrary",)),
        name="attn_sample",
    )(page_table, qlat, qpe, knew_c, knew_pt, cache_ckv, cache_kpe_t)


def _merge_body(x_ref, olat_ref, wuvbd_ref, bconv_ref, cmix_ref, sga_ref, sgb_ref, sgc_ref,
                wbra_ref, wbrb_ref, wbrc_ref, wout_ref, gffn_ref, wpq_ref, skeys_ref,
                x1_ref, h2t_ref, st_ref):
    ocat = jnp.concatenate([olat_ref[hd].astype(BF16) for hd in range(N_HEADS_A)], axis=1)
    a = _dot(ocat, wuvbd_ref[...]).astype(BF16)
    merged = (sga_ref[...] * _dot(a, wbra_ref[...])
              + sgb_ref[...] * _dot(bconv_ref[...], wbrb_ref[...])
              + sgc_ref[...] * _dot(cmix_ref[...], wbrc_ref[...]))
    x1 = x_ref[...] + _dot(merged.astype(BF16), wout_ref[...])
    x1_ref[...] = x1
    h2 = _rms_rows(x1, gffn_ref[...])
    h2t_ref[...] = h2.T.astype(BF16)
    q = _dot(h2.astype(BF16), wpq_ref[...]).astype(BF16)
    half = q.shape[1] // (2 * N_HEADS_P)
    for hc in range(2 * N_HEADS_P):
        st_ref[hc * N_KEYS:(hc + 1) * N_KEYS, :] = _dot_nt(skeys_ref[hc], q[:, hc * half:(hc + 1) * half])


def _merge(x, olat, bconv, cmix, sga, sgb, sgc, lw, tm):
    t, d = x.shape
    row = lambda w: pl.BlockSpec((tm, w), lambda i: (i, 0))
    col = lambda r: pl.BlockSpec((r, tm), lambda i: (0, i))
    n_s = 2 * N_HEADS_P * N_KEYS
    return pl.pallas_call(
        _merge_body,
        grid=(t // tm,),
        in_specs=[row(d), pl.BlockSpec((N_HEADS_A, tm, KV_RANK), lambda i: (0, i, 0)),
                  _const_spec(lw["w_uv_bd"].shape), row(W_B), row(W_C), row(d), row(d), row(d),
                  _const_spec(lw["w_br_a"].shape), _const_spec(lw["w_br_b"].shape),
                  _const_spec(lw["w_br_c"].shape), _const_spec(lw["w_out"].shape), _const_spec((1, d)),
                  _const_spec(lw["w_pq"].shape), _const_spec(lw["skeys"].shape)],
        out_specs=[row(d), col(d), col(n_s)],
        out_shape=[jax.ShapeDtypeStruct((t, d), F32), jax.ShapeDtypeStruct((d, t), BF16),
                   jax.ShapeDtypeStruct((n_s, t), F32)],
        compiler_params=_cparams(("arbitrary",)),
        name="merge",
    )(x, olat, lw["w_uv_bd"], bconv, cmix, sga, sgb, sgc, lw["w_br_a"], lw["w_br_b"], lw["w_br_c"],
      lw["w_out"], lw["g_ffn"], lw["w_pq"], lw["skeys"])


def _sort_network(n):
    pairs = []
    p = 1
    while p < n:
        k = p
        while k >= 1:
            for j in range(k % p, n - k, 2 * k):
                for i in range(min(k, n - j - k)):
                    if (i + j) // (2 * p) == (i + j + k) // (2 * p):
                        pairs.append((i + j, i + j + k))
            k //= 2
        p *= 2
    return pairs


def _top_values(slabs, n, out_ref, base):
    assert len(slabs) == n
    v = list(slabs)
    for i, j in _sort_network(n):
        v[i], v[j] = jnp.maximum(v[i], v[j]), jnp.minimum(v[i], v[j])
    for k in range(n):
        mx = jnp.max(v[0], axis=0, keepdims=True)
        out_ref[base + k:base + k + 1, :] = mx
        if k + 1 < n:
            hit = v[0] == mx
            for r in range(n - 1 - k):
                v[r] = jnp.where(hit, v[r + 1], v[r])


def _peer_select_body(st_ref, a0_ref, b1_ref, cthr_ref, top_ref, cand_ref):
    tn = st_ref.shape[1]
    n_half = N_HEADS_P * N_KEYS
    for hc in range(2 * N_HEADS_P):
        _top_values([st_ref[hc * N_KEYS + 8 * r:hc * N_KEYS + 8 * r + 8, :] for r in range(N_KEYS // 8)],
                    TOPK_P, top_ref, hc * TOPK_P)
    rank = lax.broadcasted_iota(jnp.int32, (TOPK_P, 1), 0)
    for hd in range(N_HEADS_P):
        s0 = top_ref[hd * TOPK_P:(hd + 1) * TOPK_P, :]
        s1 = top_ref[(N_HEADS_P + hd) * TOPK_P:(N_HEADS_P + hd + 1) * TOPK_P, :]
        half = TOPK_P // 2
        s0_lo = s0[0:half, :]
        part0 = s0 + s1[0:1, :]
        mid = [jnp.where(rank[0:half] < TOPK_P // (b + 1), s0_lo + s1[b:b + 1, :], -jnp.inf)
               for b in range(1, half)]
        tail = s0[0:1, :] + s1[half:TOPK_P, :]
        cands = jnp.concatenate([part0] + mid + [tail], axis=0)
        pad = jnp.full((half, tn), -jnp.inf, F32)
        _top_values([part0[0:half], part0[half:TOPK_P]] + mid + [tail] + [pad] * (TOPK_P - half - 2),
                    TOPK_P, cand_ref, 0)
        thr = cand_ref[TOPK_P - 1:TOPK_P, :]
        mx = cand_ref[0:1, :]
        z = jnp.sum(jnp.where(cands >= thr, jnp.exp(cands - mx), 0.0), axis=0, keepdims=True)
        tau_lo = jnp.full((half, tn), jnp.inf, F32)
        for b in range(1, half):
            tau_lo = jnp.minimum(tau_lo, jnp.where(mid[b - 1] >= thr, s1[b:b + 1, :], jnp.inf))
        tau_0 = jnp.min(jnp.where(tail >= thr, s1[half:TOPK_P, :], jnp.inf), axis=0, keepdims=True)
        tau_lo = jnp.minimum(tau_lo, jnp.where(rank[0:half] == 0, tau_0, jnp.inf))
        tau = jnp.minimum(jnp.where(part0 >= thr, s1[0:1, :], jnp.inf),
                          jnp.concatenate([tau_lo, jnp.full((half, tn), jnp.inf, F32)], axis=0))
        s0_all = st_ref[hd * N_KEYS:(hd + 1) * N_KEYS, :]
        s1_all = st_ref[n_half + hd * N_KEYS:n_half + (hd + 1) * N_KEYS, :]

        def factor1(x):
            return jnp.exp(x - s1[0:1, :]) * (0.5 / z)

        beta = factor1(tau)
        cthr = jnp.full((N_KEYS, tn), jnp.inf, F32)
        for a in reversed(range(TOPK_P)):
            cthr = jnp.where(s0_all >= s0[a:a + 1, :], beta[a:a + 1, :], cthr)
        cthr_ref[hd * N_KEYS:(hd + 1) * N_KEYS, :] = cthr
        a0_ref[hd * N_KEYS:(hd + 1) * N_KEYS, :] = jnp.exp(s0_all - s0[0:1, :])
        b1_ref[hd * N_KEYS:(hd + 1) * N_KEYS, :] = factor1(s1_all)


def _peer_select(st, tn):
    n_s, t = st.shape
    n_h = N_HEADS_P * N_KEYS
    col = lambda r: pl.BlockSpec((r, tn), lambda i: (0, i))
    return pl.pallas_call(
        _peer_select_body,
        grid=(t // tn,),
        in_specs=[col(n_s)],
        out_specs=[col(n_h), col(n_h), col(n_h)],
        out_shape=[jax.ShapeDtypeStruct((n_h, t), F32)] * 3,
        scratch_shapes=[pltpu.VMEM((2 * N_HEADS_P * TOPK_P, tn), F32), pltpu.VMEM((TOPK_P, tn), F32)],
        compiler_params=_cparams(("arbitrary",)),
        name="peer_select",
    )(st)


PEER_LANE_CHUNK = 512


def _peer_gate_block(item, at, a0_ref, b1_ref, cthr_ref, w_ref, keys_per_block, lane_chunk):
    tn = at.shape[1]
    for c0 in range(0, tn, lane_chunk):
        lanes = slice(c0, c0 + lane_chunk)
        for ii in range(keys_per_block):
            gate = None
            for hd in range(N_HEADS_P):
                b1 = b1_ref[hd * N_KEYS:(hd + 1) * N_KEYS, lanes]
                cthr = cthr_ref[hd, item, ii:ii + 1, lanes]
                a0 = a0_ref[hd, item, ii:ii + 1, lanes]
                term = a0 * jnp.where(b1 >= cthr, b1, 0.0)
                gate = term if gate is None else gate + term
            xa = at[ii * N_KEYS:(ii + 1) * N_KEYS, lanes]
            act2 = xa * (1.0 + lax.erf(xa * (2.0 ** -0.5)))
            w_ref[ii * N_KEYS:(ii + 1) * N_KEYS, lanes] = (gate * act2).astype(BF16)


def _peer_dense_body(eb, lane_chunk, h2t_ref, u_ref, vta_ref, vtb_ref, vtc_ref, a0_ref, b1_ref,
                     cthr_ref, out_ref, w0_ref, w1_ref):
    k = pl.program_id(1)
    nk = pl.num_programs(1)
    keys_per_block = eb // N_KEYS
    gate_args = (a0_ref, b1_ref, cthr_ref)

    @pl.when(k == 0)
    def _():
        out_ref[...] = jnp.zeros(out_ref.shape, F32)
        w1_ref[...] = jnp.zeros(w1_ref.shape, BF16)

    h2t = h2t_ref[...]
    at0 = _dot(u_ref[0:eb, :], h2t)
    out_ref[...] += _dot(vta_ref[...], w1_ref[...])
    _peer_gate_block(0, at0, *gate_args, w0_ref, keys_per_block, lane_chunk)
    at1 = _dot(u_ref[eb:2 * eb, :], h2t)
    out_ref[...] += _dot(vtb_ref[...], w0_ref[...])
    _peer_gate_block(1, at1, *gate_args, w1_ref, keys_per_block, lane_chunk)

    @pl.when(k == nk - 1)
    def _():
        out_ref[...] += _dot(vtc_ref[...], w1_ref[...])


def _peer_dense(h2t, u_bf, vt_bf, a0, b1, cthr, tn, eb):
    d, t = h2t.shape
    n_e = u_bf.shape[0]
    n_blocks = n_e // eb
    assert eb == 8 * N_KEYS
    assert n_blocks % 2 == 0
    col = lambda r: pl.BlockSpec((r, tn), lambda i, k: (0, i))
    n_h = N_HEADS_P * N_KEYS
    keys_per_block = eb // N_KEYS
    by_block = lambda a: a.reshape(N_HEADS_P, n_blocks, keys_per_block, t)
    blk_rows = pl.BlockSpec((N_HEADS_P, 2, keys_per_block, tn), lambda i, k: (0, k, 0, i))
    return pl.pallas_call(
        functools.partial(_peer_dense_body, eb, PEER_LANE_CHUNK),
        grid=(t // tn, n_blocks // 2),
        in_specs=[col(d), pl.BlockSpec((2 * eb, d), lambda i, k: (k, 0)),
                  pl.BlockSpec((d, eb), lambda i, k: (0, jnp.maximum(2 * k - 1, 0))),
                  pl.BlockSpec((d, eb), lambda i, k: (0, 2 * k)),
                  pl.BlockSpec((d, eb), lambda i, k: (0, n_blocks - 1)),
                  blk_rows, col(n_h), blk_rows],
        out_specs=col(d),
        out_shape=jax.ShapeDtypeStruct((d, t), F32),
        scratch_shapes=[pltpu.VMEM((eb, tn), BF16), pltpu.VMEM((eb, tn), BF16)],
        compiler_params=_cparams(("arbitrary", "arbitrary")),
        name="peer_dense",
    )(h2t, u_bf, vt_bf, vt_bf, vt_bf, by_block(a0), b1, by_block(cthr))


def _table_prep_body(u_ref, v_ref, ub_ref, vt_ref):
    ub_ref[...] = u_ref[...].astype(BF16)
    vt_ref[...] = v_ref[...].T.astype(BF16)


def _table_prep(layer, u_tab, v_tab, rows):
    _, n_e, d = u_tab.shape
    src = pl.BlockSpec((None, rows, d), lambda e: (layer, e, 0))
    return pl.pallas_call(
        _table_prep_body,
        grid=(n_e // rows,),
        in_specs=[src, src],
        out_specs=[pl.BlockSpec((rows, d), lambda e: (e, 0)), pl.BlockSpec((d, rows), lambda e: (0, e))],
        out_shape=[jax.ShapeDtypeStruct((n_e, d), BF16), jax.ShapeDtypeStruct((d, n_e), BF16)],
        compiler_params=_cparams(("arbitrary",)),
        name="table_prep",
    )(u_tab, v_tab)


def _ple_body(last, x1_ref, pt_ref, gple_ref, wpg_ref, p_ref, wple_ref, *rest):
    if last:
        gfin_ref, x3_ref, y_ref = rest
    else:
        (x3_ref,) = rest
    x2 = x1_ref[...] + pt_ref[...].T
    hp = _rms_rows(x2, gple_ref[...]).astype(BF16)
    gate = jax.nn.sigmoid(_dot(hp, wpg_ref[...]))
    x3 = x2 + gate * _dot(p_ref[...].astype(BF16), wple_ref[...])
    x3_ref[...] = x3
    if last:
        y_ref[...] = _rms_rows(x3, gfin_ref[...])


def _ple(x1, peer_t, col_off, p_i, lw, g_final, last, tm):
    t, d = x1.shape
    row = lambda w: pl.BlockSpec((tm, w), lambda i: (i, 0))
    in_specs = [row(d), pl.BlockSpec((d, tm), lambda i: (0, i + col_off)), _const_spec((1, d)),
                _const_spec(lw["w_pg"].shape), row(p_i.shape[1]), _const_spec(lw["w_ple"].shape)]
    args = [x1, peer_t, lw["g_ple"], lw["w_pg"], p_i, lw["w_ple"]]
    out_specs = [row(d)]
    out_shape = [jax.ShapeDtypeStruct((t, d), F32)]
    if last:
        in_specs.append(_const_spec((1, d)))
        args.append(g_final)
        out_specs.append(row(d))
        out_shape.append(jax.ShapeDtypeStruct((t, d), F32))
    return pl.pallas_call(
        functools.partial(_ple_body, last),
        grid=(t // tm,),
        in_specs=in_specs,
        out_specs=out_specs,
        out_shape=out_shape,
        compiler_params=_cparams(("arbitrary",)),
        name="ple_last" if last else "ple",
    )(*args)


def _rope_tables(pos):
    inv = 1.0 / (ROPE_THETA ** (jnp.arange(0, ROPE_DIM, 2, dtype=F32) / ROPE_DIM))
    ang = pos.astype(F32)[:, None] * inv[None, :]
    cos, sin = jnp.cos(ang), jnp.sin(ang)
    cosk = jnp.concatenate([cos, cos], axis=1)
    sink = jnp.concatenate([-sin, sin], axis=1)
    return {"cosk": cosk, "sink": sink, "cosq": jnp.tile(cosk, (1, N_HEADS_A)),
            "sinq": jnp.tile(sink, (1, N_HEADS_A))}


def _swap_halves(w):
    half = w.shape[-1] // 2
    return jnp.concatenate([w[..., half:], w[..., :half]], axis=-1)


def _block_diag(blocks):
    n, r, c = blocks.shape
    eye = jnp.eye(n, dtype=blocks.dtype)
    return (eye[:, None, :, None] * blocks[:, :, None, :]).reshape(n * r, n * c)


def _layer_weights(i, d, dec_seq, g_attn, w_in, g_q, g_kv, w_uq, w_uk, w_uv, w_conv, g_v, b_v, w_s, b_s,
                   w_br_a, w_br_b, w_br_c, w_out, g_ffn, w_pq, sub_keys, u_tab, v_tab, g_ple, w_pg, w_ple):
    bf = lambda a: a.astype(BF16)
    wi = w_in[i]
    o_kpe = Q_RANK + KV_RANK
    o_cb = o_kpe + ROPE_DIM
    o_g = o_cb + 3 * W_B + 2 * W_C
    w_kpe = wi[:, o_kpe:o_cb]
    w_small = jnp.concatenate([wi[:, :o_kpe], wi[:, o_cb:o_g], w_kpe, _swap_halves(w_kpe)], axis=1)
    uq = w_uq[i]
    uq_pe = uq[:, :, NOPE_DIM:]
    w_uq_p = jnp.concatenate([uq[:, :, :NOPE_DIM].reshape(Q_RANK, -1), uq_pe.reshape(Q_RANK, -1),
                              _swap_halves(uq_pe).reshape(Q_RANK, -1)], axis=1)
    ws = jnp.tril(w_s[i])
    ws_small = jnp.tril(w_s[i][:, :dec_seq, :dec_seq])
    eye = jnp.eye(CHUNK // dec_seq, dtype=F32)
    ms_s = jnp.einsum("ab,gts->gatbs", eye, ws_small).reshape(N_GROUPS_C, CHUNK, CHUNK)
    gw = W_C // N_GROUPS_C
    bs_p = jnp.repeat(b_s[i].T, gw, axis=1)
    bs_s = jnp.tile(bs_p[:dec_seq], (CHUNK // dec_seq, 1))
    return {
        "g_attn": g_attn[i][None], "w_small": bf(w_small), "w_gates": bf(wi[:, o_g:]),
        "g_q": g_q[i][None], "g_kv": g_kv[i][None], "w_uq": bf(w_uq_p),
        "w_uk_bd": bf(_block_diag(jnp.transpose(w_uk[i], (1, 2, 0)))),
        "w_uv_bd": bf(_block_diag(jnp.transpose(w_uv[i], (1, 0, 2)))),
        "w_conv": w_conv[i], "g_v": g_v[i][None], "b_v": b_v[i][None],
        "ms_p": bf(ws), "ms_s": bf(ms_s), "bs_p": bs_p, "bs_s": bs_s,
        "w_br_a": bf(w_br_a[i]), "w_br_b": bf(w_br_b[i]), "w_br_c": bf(w_br_c[i]), "w_out": bf(w_out[i]),
        "g_ffn": g_ffn[i][None], "w_pq": bf(jnp.transpose(w_pq[i], (0, 2, 1, 3)).reshape(d, -1)),
        "skeys": bf(jnp.transpose(sub_keys[i], (1, 0, 2, 3)).reshape(2 * N_HEADS_P, N_KEYS, -1)),
        "g_ple": g_ple[i][None], "w_pg": bf(w_pg[i]), "w_ple": bf(w_ple[i]),
    }


def _pick_tile(n, prefs):
    for p in prefs:
        if n % p == 0:
            return p
    raise ValueError(f"no tile for {n}")


def kernel(x_prompt, x_sample, p_prompt, p_sample, cache_ckv, cache_kpe, state_conv, page_table,
           g_attn, w_in, g_q, g_kv, w_uq, w_uk, w_uv, w_conv, g_v, b_v, w_s, b_s,
           w_br_a, w_br_b, w_br_c, w_out, g_ffn, w_pq, sub_keys, u_tab, v_tab,
           g_ple, w_pg, w_ple, g_final):
    batch, seq, d = x_prompt.shape
    dec_b, dec_seq, _ = x_sample.shape
    depth = w_in.shape[0]
    n_pages = page_table.shape[1]
    page = cache_ckv.shape[2]
    past_len = n_pages * page
    tp, ts = batch * seq, dec_b * dec_seq
    assert seq % CHUNK == 0 and CHUNK % dec_seq == 0 and ts % CHUNK == 0 and dec_seq == 8
    assert page == CHUNK

    tm_p = _pick_tile(seq, (256, 128))
    tm_s = _pick_tile(ts, (256, 128))
    kb = _pick_tile(seq, (512, 256, 128))
    tn_p = _pick_tile(tp, (512, 256, 128))
    tn_s = _pick_tile(ts, (512, 256, 128))
    eb = 1024

    tabs_p = _rope_tables(jnp.arange(seq, dtype=jnp.int32))
    tabs_s = _rope_tables(past_len + (jnp.arange(tm_s, dtype=jnp.int32) % dec_seq))
    g_fin = g_final[None]
    cache_kpe_t = jnp.swapaxes(cache_kpe, 2, 3)

    xp = x_prompt.reshape(tp, d)
    xs = x_sample.reshape(ts, d)
    outs = {k: [] for k in ("ckv_p", "kpe_p", "ckv_s", "kpe_s", "conv_p", "conv_s", "v_s")}
    y_p = y_s = None
    for i in range(depth):
        lw = _layer_weights(i, d, dec_seq, g_attn, w_in, g_q, g_kv, w_uq, w_uk, w_uv, w_conv, g_v, b_v,
                            w_s, b_s, w_br_a, w_br_b, w_br_c, w_out, g_ffn, w_pq, sub_keys, u_tab, v_tab,
                            g_ple, w_pg, w_ple)
        last = i == depth - 1

        (qlt_p, qpt_p, ckv_p, kpe_p, kcb_p, kpb_p, bconv_p, xc_p, cmix_p, _vn_p, sga_p, sgb_p, sgc_p,
         kct_p) = _inproj(xp, lw, tabs_p, False, seq, None, tm_p)
        olat_p = _attn_prompt(qlt_p, qpt_p, kcb_p, kpb_p, kct_p, batch, seq, kb)
        x1_p, h2t_p, st_p = _merge(xp, olat_p, bconv_p, cmix_p, sga_p, sgb_p, sgc_p, lw, tm_p)

        st_c = state_conv[i]
        zero_row = jnp.zeros((dec_b, 1, W_B), F32)
        f1 = jnp.concatenate([st_c[:, 1:2], jnp.zeros((dec_b, dec_seq - 1, W_B), F32)], axis=1).reshape(ts, W_B)
        f2 = jnp.concatenate([st_c[:, 0:1], st_c[:, 1:2], jnp.zeros((dec_b, dec_seq - 2, W_B), F32)],
                             axis=1).reshape(ts, W_B)
        del zero_row
        (qlat_s, qpe_s, ckv_s, kpe_s, kcb_s, kpb_s, bconv_s, xc_s, cmix_s, vn_s, sga_s, sgb_s, sgc_s) = _inproj(
            xs, lw, tabs_s, True, dec_seq, (f1, f2), tm_s)
        knew_c = jnp.pad(kcb_s.reshape(dec_b, dec_seq, KV_RANK), ((0, 0), (0, page - dec_seq), (0, 0)))
        knew_p = jnp.pad(jnp.swapaxes(kpb_s.reshape(dec_b, dec_seq, ROPE_DIM), 1, 2),
                         ((0, 0), (0, 0), (0, page - dec_seq)))
        olat_s = _attn_sample_dma(i, qlat_s, qpe_s, cache_ckv, cache_kpe_t, page_table, knew_c, knew_p, dec_seq)
        x1_s, h2t_s, st_s = _merge(xs, olat_s, bconv_s, cmix_s, sga_s, sgb_s, sgc_s, lw, tm_s)

        u_bf, vt_bf = _table_prep(i, u_tab, v_tab, eb)
        peer_p = _peer_dense(h2t_p, u_bf, vt_bf, *_peer_select(st_p, tn_p), tn_p, eb)
        peer_s = _peer_dense(h2t_s, u_bf, vt_bf, *_peer_select(st_s, tn_s), tn_s, eb)

        res_p = _ple(x1_p, peer_p, 0, p_prompt[i].reshape(tp, -1), lw, g_fin, last, tm_p)
        res_s = _ple(x1_s, peer_s, 0, p_sample[i].reshape(ts, -1), lw, g_fin, last, tm_s)
        xp, xs = res_p[0], res_s[0]
        if last:
            y_p, y_s = res_p[1], res_s[1]

        outs["ckv_p"].append(ckv_p.reshape(batch, seq, KV_RANK))
        outs["kpe_p"].append(kpe_p.reshape(batch, seq, ROPE_DIM))
        outs["ckv_s"].append(ckv_s.reshape(dec_b, dec_seq, KV_RANK))
        outs["kpe_s"].append(kpe_s.reshape(dec_b, dec_seq, ROPE_DIM))
        outs["conv_p"].append(xc_p.reshape(batch, seq, W_B)[:, seq - (CONV_W - 1):])
        outs["conv_s"].append(xc_s.reshape(dec_b, dec_seq, W_B)[:, dec_seq - (CONV_W - 1):])
        outs["v_s"].append(vn_s.reshape(dec_b, dec_seq, W_C))

    return (y_p.reshape(batch, seq, d), y_s.reshape(dec_b, dec_seq, d),
            jnp.stack(outs["ckv_p"]), jnp.stack(outs["kpe_p"]), jnp.stack(outs["ckv_s"]),
            jnp.stack(outs["kpe_s"]), jnp.stack(outs["conv_p"]), jnp.stack(outs["conv_s"]),
            jnp.stack(outs["v_s"]))
```
